```python
import math
import jax, jax.numpy as jnp
from jax import lax
import numpy as np

D_MODEL = 1024
BATCH = 8
SEQ = 2048
DEPTH = 2

CHUNK = 64
N_META = 16
Q_BLOCK = 128
ROPE_THETA = 500000.0
EPS = 1e-6
N_BRANCH = 3

SB_HEADS = D_MODEL // 128
SB_HEAD_DIM = 64
SB_WIDTH = SB_HEADS * SB_HEAD_DIM

MLA_HEADS = D_MODEL // 128
MLA_NOPE = 64
MLA_ROPE = 32
MLA_V = 64
MLA_Q_RANK = 3 * D_MODEL // 8
MLA_KV_RANK = D_MODEL // 4
MLA_WIDTH = MLA_HEADS * MLA_V

DIFF_HEADS = D_MODEL // 256
DIFF_HEAD_DIM = 64
DIFF_V_DIM = 2 * DIFF_HEAD_DIM
DIFF_WIDTH = DIFF_HEADS * DIFF_V_DIM
ROT_DIM = DIFF_HEAD_DIM // 4

IN_SIZES = (
    SB_WIDTH, SB_WIDTH, SB_WIDTH, SB_WIDTH,
    MLA_Q_RANK, MLA_KV_RANK, MLA_ROPE, MLA_WIDTH,
    2 * DIFF_HEADS * DIFF_HEAD_DIM, 2 * DIFF_HEADS * DIFF_HEAD_DIM,
    DIFF_HEADS * DIFF_V_DIM, DIFF_WIDTH,
    N_BRANCH * D_MODEL,
)
IN_COLS = sum(IN_SIZES)

kernel_name = "hybrid_stickbreak_mla_diffattn_gated_merge"


def _rmsnorm(x, g):
    x32 = x.astype(jnp.float32)
    y = x32 * lax.rsqrt(jnp.mean(x32 * x32, axis=-1, keepdims=True) + EPS)
    return y.astype(x.dtype) * g


def _rope_tables(n_pos, dim):
    inv_freq = ROPE_THETA ** (-jnp.arange(0, dim, 2, dtype=jnp.float32) / dim)
    ang = jnp.arange(n_pos, dtype=jnp.float32)[:, None] * inv_freq[None, :]
    return jnp.cos(ang), jnp.sin(ang)


def _rope(x, cos, sin):
    half = x.shape[-1] // 2
    x32 = x.astype(jnp.float32)
    x1, x2 = x32[..., :half], x32[..., half:]
    out = jnp.concatenate([x1 * cos - x2 * sin, x2 * cos + x1 * sin], axis=-1)
    return out.astype(x.dtype)


def _partial_rope(x, cos, sin):
    return jnp.concatenate([_rope(x[..., :ROT_DIM], cos, sin), x[..., ROT_DIM:]], axis=-1)


def _heads(t, n_heads):
    b, l, w = t.shape
    return t.reshape(b, l, n_heads, w // n_heads).transpose(0, 2, 1, 3)


def _merge_heads(t):
    b, h, l, d = t.shape
    return t.transpose(0, 2, 1, 3).reshape(b, l, h * d)


def _sweep(block_fn, n_blocks):
    return jnp.concatenate([block_fn(i) for i in range(n_blocks)], axis=2)


def _chunk_mask(chunk_ids, q0, k_end):
    qc = chunk_ids[q0:q0 + Q_BLOCK]
    kc = chunk_ids[:k_end]
    return kc[None, :] <= qc[:, None]


def _masked_softmax(s, mask):
    return jax.nn.softmax(jnp.where(mask, s, -jnp.inf), axis=-1)


def _layer(x, layer_idx, pos, chunk_ids, cos_d, sin_d, cos_m, sin_m,
           norm_g, w_in, b_gate, mla_cq_g, mla_ckv_g, mla_w_uq, mla_w_ukv,
           diff_lambda, diff_norm_g, w_o_sb, w_o_mla, w_o_diff, w_out):
    B, L, D = x.shape
    n_blocks = L // Q_BLOCK
    h = _rmsnorm(x, norm_g)
    proj = h @ w_in
    offsets = []
    acc = 0
    for s in IN_SIZES[:-1]:
        acc += s
        offsets.append(acc)
    (sb_q, sb_k, sb_v, sb_z, mla_cq, mla_ckv, mla_kr, mla_z,
     d_q, d_k, d_v, d_z, gate_logits) = jnp.split(proj, offsets, axis=-1)

    q_a, k_a, v_a = _heads(sb_q, SB_HEADS), _heads(sb_k, SB_HEADS), _heads(sb_v, SB_HEADS)
    scale_a = 1.0 / math.sqrt(SB_HEAD_DIM)

    def sb_block(i):
        q0 = i * Q_BLOCK
        k_end = q0 + Q_BLOCK
        z = jnp.einsum('bhqd,bhkd->bhqk', q_a[:, :, q0:k_end], k_a[:, :, :k_end]).astype(jnp.float32) * scale_a
        valid = pos[:k_end][None, :] < pos[q0:k_end][:, None]
        log_keep = jnp.where(valid, -jax.nn.softplus(z), 0.0)
        after = lax.cumsum(log_keep, axis=3, reverse=True) - log_keep
        w = jnp.where(valid, jnp.exp(jax.nn.log_sigmoid(z) + after), 0.0)
        return jnp.einsum('bhqk,bhkd->bhqd', w.astype(v_a.dtype), v_a[:, :, :k_end])

    o_sb = _merge_heads(_sweep(sb_block, n_blocks))

    q_b = _heads(_rmsnorm(mla_cq, mla_cq_g) @ mla_w_uq, MLA_HEADS)
    q_b = jnp.concatenate([q_b[..., :MLA_NOPE], _rope(q_b[..., MLA_NOPE:], cos_m, sin_m)], axis=-1)
    kv_b = _heads(_rmsnorm(mla_ckv, mla_ckv_g) @ mla_w_ukv, MLA_HEADS)
    k_rope = _rope(mla_kr, cos_m, sin_m)
    k_b = jnp.concatenate([kv_b[..., :MLA_NOPE],
                           jnp.broadcast_to(k_rope[:, None], (B, MLA_HEADS, L, MLA_ROPE))], axis=-1)
    v_b = kv_b[..., MLA_NOPE:]
    scale_b = 1.0 / math.sqrt(MLA_NOPE + MLA_ROPE)

    def mla_block(i):
        q0 = i * Q_BLOCK
        k_end = min(q0 + Q_BLOCK + CHUNK, L)
        s = jnp.einsum('bhqd,bhkd->bhqk', q_b[:, :, q0:q0 + Q_BLOCK], k_b[:, :, :k_end]).astype(jnp.float32) * scale_b
        p = _masked_softmax(s, _chunk_mask(chunk_ids, q0, k_end))
        return jnp.einsum('bhqk,bhkd->bhqd', p.astype(v_b.dtype), v_b[:, :, :k_end])

    o_mla = _merge_heads(_sweep(mla_block, n_blocks))

    q_c = _partial_rope(d_q.reshape(B, L, DIFF_HEADS, 2, DIFF_HEAD_DIM).transpose(0, 2, 3, 1, 4), cos_d, sin_d)
    k_c = _partial_rope(d_k.reshape(B, L, DIFF_HEADS, 2, DIFF_HEAD_DIM).transpose(0, 2, 3, 1, 4), cos_d, sin_d)
    v_c = _heads(d_v, DIFF_HEADS)
    lam_init = 0.8 - 0.6 * math.exp(-0.3 * layer_idx)
    lam32 = diff_lambda.astype(jnp.float32)
    lam = (jnp.exp(jnp.sum(lam32[0] * lam32[1])) - jnp.exp(jnp.sum(lam32[2] * lam32[3])) + lam_init)
    scale_c = 1.0 / math.sqrt(DIFF_HEAD_DIM)

    def diff_block(i):
        q0 = i * Q_BLOCK
        k_end = min(q0 + Q_BLOCK + CHUNK, L)
        s = jnp.einsum('bhmqd,bhmkd->bhmqk', q_c[:, :, :, q0:q0 + Q_BLOCK], k_c[:, :, :, :k_end]).astype(jnp.float32) * scale_c
        p = _masked_softmax(s, _chunk_mask(chunk_ids, q0, k_end))
        w = p[:, :, 0] - lam * p[:, :, 1]
        return jnp.einsum('bhqk,bhkd->bhqd', w.astype(v_c.dtype), v_c[:, :, :k_end])

    o_c = _sweep(diff_block, n_blocks)
    o_diff = _merge_heads(_rmsnorm(o_c, diff_norm_g) * (1.0 - lam_init))

    y_sb = (o_sb * jax.nn.silu(sb_z)) @ w_o_sb
    y_mla = (o_mla * jax.nn.silu(mla_z)) @ w_o_mla
    y_diff = (o_diff * jax.nn.silu(d_z)) @ w_o_diff
    g = jax.nn.sigmoid(gate_logits + b_gate).reshape(B, L, N_BRANCH, D)
    merged = g[:, :, 0] * y_sb + g[:, :, 1] * y_mla + g[:, :, 2] * y_diff
    return x + merged @ w_out


def setup_inputs(seed: int = 0) -> dict:
    key = jax.random.key(seed)
    ks = jax.random.split(key, 18)

    def nrm(k, shape, fan_in):
        return jax.random.normal(k, shape, jnp.float32) * fan_in ** -0.5

    def gain(k, shape):
        return 1.0 + 0.05 * jax.random.normal(k, shape, jnp.float32)

    return {
        "x": jax.random.normal(ks[0], (BATCH, SEQ, D_MODEL), jnp.float32),
        "meta_tokens": jax.random.normal(ks[1], (N_META, D_MODEL), jnp.float32),
        "norm_g": gain(ks[2], (DEPTH, D_MODEL)),
        "w_in": nrm(ks[3], (DEPTH, D_MODEL, IN_COLS), D_MODEL),
        "b_gate": 0.01 * jax.random.normal(ks[4], (DEPTH, N_BRANCH * D_MODEL), jnp.float32),
        "mla_cq_g": gain(ks[5], (DEPTH, MLA_Q_RANK)),
        "mla_ckv_g": gain(ks[6], (DEPTH, MLA_KV_RANK)),
        "mla_w_uq": nrm(ks[7], (DEPTH, MLA_Q_RANK, MLA_HEADS * (MLA_NOPE + MLA_ROPE)), MLA_Q_RANK),
        "mla_w_ukv": nrm(ks[8], (DEPTH, MLA_KV_RANK, MLA_HEADS * (MLA_NOPE + MLA_V)), MLA_KV_RANK),
        "diff_lambda": 0.1 * jax.random.normal(ks[9], (DEPTH, 4, DIFF_HEAD_DIM), jnp.float32),
        "diff_norm_g": gain(ks[10], (DEPTH, DIFF_V_DIM)),
        "w_o_sb": nrm(ks[11], (DEPTH, SB_WIDTH, D_MODEL), SB_WIDTH),
        "w_o_mla": nrm(ks[12], (DEPTH, MLA_WIDTH, D_MODEL), MLA_WIDTH),
        "w_o_diff": nrm(ks[13], (DEPTH, DIFF_WIDTH, D_MODEL), DIFF_WIDTH),
        "w_out": nrm(ks[14], (DEPTH, D_MODEL, D_MODEL), D_MODEL),
        "final_g": gain(ks[15], (D_MODEL,)),
    }


def reference(x, meta_tokens, norm_g, w_in, b_gate, mla_cq_g, mla_ckv_g, mla_w_uq, mla_w_ukv,
              diff_lambda, diff_norm_g, w_o_sb, w_o_mla, w_o_diff, w_out, final_g):
    B, S, D = x.shape
    L = S + N_META
    L_pad = -(-L // Q_BLOCK) * Q_BLOCK
    meta = jnp.broadcast_to(meta_tokens.astype(x.dtype)[None], (B, N_META, D))
    h = jnp.concatenate([meta, x, jnp.zeros((B, L_pad - L, D), x.dtype)], axis=1)
    pos = jnp.arange(L_pad)
    chunk_ids = jnp.where(pos < N_META, 0, (pos - N_META) // CHUNK + 1)
    cos_d, sin_d = _rope_tables(L_pad, ROT_DIM)
    cos_m, sin_m = _rope_tables(L_pad, MLA_ROPE)
    for l in range(DEPTH):
        h = _layer(h, l, pos, chunk_ids, cos_d, sin_d, cos_m, sin_m,
                   norm_g[l], w_in[l], b_gate[l], mla_cq_g[l], mla_ckv_g[l], mla_w_uq[l], mla_w_ukv[l],
                   diff_lambda[l], diff_norm_g[l], w_o_sb[l], w_o_mla[l], w_o_diff[l], w_out[l])
    h = _rmsnorm(h, final_g)
    return h[:, N_META:N_META + S]
```

```python
import functools
import math

import jax
import jax.numpy as jnp
from jax import lax
from jax.experimental import pallas as pl
from jax.experimental.pallas import tpu as pltpu

F32 = jnp.float32
BF16 = jnp.bfloat16

D_MODEL = 1024
CHUNK = 64
N_META = 16
ROPE_THETA = 500000.0
EPS = 1e-6
LANES = 128
HEAD = 64
WIDTH = 512
MLA_HEADS = 8
MLA_ROPE = 32
MLA_Q_RANK = 384
MLA_KV_RANK = 256
DIFF_ROT = 16

META_ROWS = 256
ROW_TILE = 256
ATT_TILE = 256
NEG_BIG = -1e30
VMEM_LIMIT = 56 * 1024 * 1024

C_GATE = 0
C_SBQ = 3072
C_SBK = 3584
C_SBV = 4096
C_SBZ = 4608
C_DQ = 5120
C_DK = 5632
C_DV = 6144
C_DZ = 6656
C_MZ = 7168
C_LAT = 7680
LAT_W = 768
N_PROJ = 8448
PROJ_TN = 768


def _params(sem):
    return pltpu.CompilerParams(dimension_semantics=sem, vmem_limit_bytes=VMEM_LIMIT)


def _rms(x32, g):
    ms = jnp.mean(x32 * x32, axis=-1, keepdims=True)
    return x32 * lax.rsqrt(ms + EPS) * g


def _norm_kernel(h_ref, g_ref, o_ref):
    o_ref[...] = _rms(h_ref[...], g_ref[...]).astype(o_ref.dtype)


def _norm_call(h, g, tm):
    rows = h.shape[0]
    return pl.pallas_call(
        _norm_kernel,
        grid=(rows // tm,),
        in_specs=[pl.BlockSpec((tm, D_MODEL), lambda i: (i, 0)),
                  pl.BlockSpec((1, D_MODEL), lambda i: (0, 0))],
        out_specs=pl.BlockSpec((tm, D_MODEL), lambda i: (i, 0)),
        out_shape=jax.ShapeDtypeStruct((rows, D_MODEL), BF16),
        compiler_params=_params(("parallel",)),
        name="norm_in",
    )(h, g)


def _proj_kernel(a_ref, w_ref, o_ref):
    o_ref[...] = jnp.dot(a_ref[...], w_ref[...], preferred_element_type=F32).astype(o_ref.dtype)


def _proj_call(hn, w, tm):
    rows = hn.shape[0]
    return pl.pallas_call(
        _proj_kernel,
        grid=(rows // tm, N_PROJ // PROJ_TN),
        in_specs=[pl.BlockSpec((tm, D_MODEL), lambda i, j: (i, 0)),
                  pl.BlockSpec((D_MODEL, PROJ_TN), lambda i, j: (0, j))],
        out_specs=pl.BlockSpec((tm, PROJ_TN), lambda i, j: (i, j)),
        out_shape=jax.ShapeDtypeStruct((rows, N_PROJ), BF16),
        compiler_params=_params(("parallel", "arbitrary")),
        name="in_proj",
    )(hn, w)


def _rot(x, c, s1, s2, shift):
    return x * c + pltpu.roll(x, LANES - shift, 1) * s1 + pltpu.roll(x, shift, 1) * s2


def _prep_kernel(lat_ref, dq_ref, dk_ref, gq_ref, gkv_ref, wuq_ref, wuk_ref, wuv_ref,
                 cm_ref, s1m_ref, s2m_ref, cd_ref, s1d_ref, s2d_ref,
                 qm_ref, km_ref, vm_ref, dqo_ref, dko_ref):
    lat = lat_ref[...].astype(F32)
    ckv = lat[:, :MLA_KV_RANK]
    cq = lat[:, MLA_KV_RANK:MLA_KV_RANK + MLA_Q_RANK]
    kr = lat[:, MLA_KV_RANK + MLA_Q_RANK:]
    ncq = _rms(cq, gq_ref[...]).astype(BF16)
    nckv = _rms(ckv, gkv_ref[...]).astype(BF16)
    cm, s1m, s2m = cm_ref[...], s1m_ref[...], s2m_ref[...]
    scale_b = 1.0 / math.sqrt(HEAD + MLA_ROPE)
    k_rope = _rot(kr, cm, s1m, s2m, MLA_ROPE // 2)
    vm_ref[...] = jnp.dot(nckv, wuv_ref[...], preferred_element_type=F32).astype(BF16)
    for h in range(MLA_HEADS):
        sl = slice(h * LANES, (h + 1) * LANES)
        qf = jnp.dot(ncq, wuq_ref[:, sl], preferred_element_type=F32)
        qm_ref[:, sl] = (_rot(qf, cm, s1m, s2m, MLA_ROPE // 2) * scale_b).astype(BF16)
        kf = jnp.dot(nckv, wuk_ref[:, sl], preferred_element_type=F32)
        km_ref[:, sl] = (kf + k_rope).astype(BF16)
    cd, s1d, s2d = cd_ref[...], s1d_ref[...], s2d_ref[...]
    for h in range(WIDTH // LANES):
        sl = slice(h * LANES, (h + 1) * LANES)
        dqo_ref[:, sl] = _rot(dq_ref[:, sl].astype(F32), cd, s1d, s2d, DIFF_ROT // 2).astype(BF16)
        dko_ref[:, sl] = _rot(dk_ref[:, sl].astype(F32), cd, s1d, s2d, DIFF_ROT // 2).astype(BF16)


def _prep_call(proj, gq, gkv, wuq, wuk, wuv, tabs, tm, tok_rows, seq):
    rows = proj.shape[0]
    n_tok_tiles = tok_rows // tm
    per_seq = seq // tm

    def tab_map(i):
        return (jnp.where(i < n_tok_tiles, i % per_seq, per_seq), 0)

    row = lambda w, c: pl.BlockSpec((tm, w), lambda i: (i, c))
    full = lambda a: pl.BlockSpec(a.shape, lambda i: (0, 0))
    tab = pl.BlockSpec((tm, LANES), tab_map)
    out = lambda w: jax.ShapeDtypeStruct((rows, w), BF16)
    return pl.pallas_call(
        _prep_kernel,
        grid=(rows // tm,),
        in_specs=[row(LAT_W, C_LAT // LAT_W), row(WIDTH, C_DQ // WIDTH), row(WIDTH, C_DK // WIDTH),
                  full(gq), full(gkv), full(wuq), full(wuk), full(wuv)] + [tab] * 6,
        out_specs=[row(MLA_HEADS * LANES, 0), row(MLA_HEADS * LANES, 0), row(WIDTH, 0),
                   row(WIDTH, 0), row(WIDTH, 0)],
        out_shape=[out(MLA_HEADS * LANES), out(MLA_HEADS * LANES), out(WIDTH), out(WIDTH), out(WIDTH)],
        compiler_params=_params(("parallel",)),
        name="prep",
    )(proj, proj, proj, gq, gkv, wuq, wuk, wuv, *tabs)


def _sigmoid(x):
    return 1.0 / (1.0 + jnp.exp(-x))


def _merge_kernel(osb_ref, omla_ref, odiff_ref, zsb_ref, zmla_ref, zdiff_ref, g0_ref, g1_ref, g2_ref,
                  h_ref, bg_ref, wsb_ref, wmla_ref, wdiff_ref, wout_ref, gn_ref, *out_refs, last):
    def branch(o_ref, z_ref, w_ref):
        z = z_ref[...].astype(F32)
        a = o_ref[...].astype(F32) * (z * _sigmoid(z))
        return jnp.dot(a.astype(BF16), w_ref[...], preferred_element_type=F32)

    bg = bg_ref[...]
    merged = _sigmoid(g0_ref[...].astype(F32) + bg[:, :D_MODEL]) * branch(osb_ref, zsb_ref, wsb_ref)
    merged += _sigmoid(g1_ref[...].astype(F32) + bg[:, D_MODEL:2 * D_MODEL]) * branch(omla_ref, zmla_ref, wmla_ref)
    merged += _sigmoid(g2_ref[...].astype(F32) + bg[:, 2 * D_MODEL:]) * branch(odiff_ref, zdiff_ref, wdiff_ref)
    h_new = h_ref[...] + jnp.dot(merged.astype(BF16), wout_ref[...], preferred_element_type=F32)
    normed = _rms(h_new, gn_ref[...])
    if last:
        out_refs[0][...] = normed
    else:
        out_refs[0][...] = h_new
        out_refs[1][...] = normed.astype(BF16)


def _merge_call(o_sb, o_mla, o_diff, proj, h, bg, w_sb, w_mla, w_diff, w_out, g_next, tm, out_rows, last):
    row = lambda w, c: pl.BlockSpec((tm, w), lambda i: (i, c))
    full = lambda a: pl.BlockSpec(a.shape, lambda i: (0, 0))
    if last:
        out_specs = [row(D_MODEL, 0)]
        out_shape = [jax.ShapeDtypeStruct((out_rows, D_MODEL), F32)]
    else:
        out_specs = [row(D_MODEL, 0), row(D_MODEL, 0)]
        out_shape = [jax.ShapeDtypeStruct((out_rows, D_MODEL), F32),
                     jax.ShapeDtypeStruct((out_rows, D_MODEL), BF16)]
    return pl.pallas_call(
        functools.partial(_merge_kernel, last=last),
        grid=(out_rows // tm,),
        in_specs=[row(WIDTH, 0), row(WIDTH, 0), row(WIDTH, 0),
                  row(WIDTH, C_SBZ // WIDTH), row(WIDTH, C_MZ // WIDTH), row(WIDTH, C_DZ // WIDTH),
                  row(D_MODEL, 0), row(D_MODEL, 1), row(D_MODEL, 2),
                  row(D_MODEL, 0), full(bg), full(w_sb), full(w_mla), full(w_diff), full(w_out), full(g_next)],
        out_specs=out_specs,
        out_shape=out_shape,
        compiler_params=_params(("parallel",)),
        name="merge_last" if last else "merge",
    )(o_sb, o_mla, o_diff, proj, proj, proj, proj, proj, proj, h, bg, w_sb, w_mla, w_diff, w_out, g_next)


def _qk(q, k):
    return lax.dot_general(q, k, (((1,), (1,)), ((), ())), preferred_element_type=F32)


def _iota2(shape, axis):
    return lax.broadcasted_iota(jnp.int32, shape, axis)


def _half_mask(rows, half):
    lane = _iota2((rows, LANES), 1)
    return (lane >= HEAD) if half else (lane < HEAD)


def _sb_tile(qh, kb, vb, tri, carry, mask):
    run, acc = carry
    z = _qk(qh, kb)
    lk = -(jnp.maximum(z, 0.0) + jnp.log1p(jnp.exp(-jnp.abs(z))))
    if mask is not None:
        lk = jnp.where(mask, lk, 0.0)
    hi = lk.astype(BF16)
    lo = (lk - hi.astype(F32)).astype(BF16)
    after = jnp.dot(hi, tri, preferred_element_type=F32) + jnp.dot(lo, tri, preferred_element_type=F32)
    logw = z + lk + after + run
    if mask is not None:
        logw = jnp.where(mask, logw, NEG_BIG)
    w = jnp.exp(logw)
    acc = acc + jnp.dot(w.astype(BF16), vb, preferred_element_type=F32)
    run = run + jnp.sum(lk, axis=-1, keepdims=True)
    return run, acc


def _strict_tri(n):
    return (_iota2((n, n), 0) > _iota2((n, n), 1)).astype(BF16)


def _sb_kernel(q_ref, k_ref, v_ref, km_ref, vm_ref, o_ref, *, n_tok_tiles):
    i = pl.program_id(2)
    t = ATT_TILE
    q = q_ref[...] * 0.125
    tri = _strict_tri(t)
    tri_m = _strict_tri(N_META)
    causal = _iota2((t, t), 1) < _iota2((t, t), 0)
    causal_m = _iota2((t, N_META), 1) < _iota2((t, N_META), 0)
    zero = jnp.zeros_like(q)
    init = (jnp.zeros((t, 1), F32), jnp.zeros((t, LANES), F32))

    def head(half):
        qh = jnp.where(_half_mask(t, half), q, zero)

        def tok():
            start = pl.multiple_of(i * t, t)
            carry = _sb_tile(qh, k_ref[pl.ds(start, t), :], v_ref[pl.ds(start, t), :], tri, init, causal)

            def body(s, c):
                st = pl.multiple_of((i - 1 - s) * t, t)
                return _sb_tile(qh, k_ref[pl.ds(st, t), :], v_ref[pl.ds(st, t), :], tri, c, None)

            carry = lax.fori_loop(0, i, body, carry)
            return _sb_tile(qh, km_ref[...], vm_ref[...], tri_m, carry, None)[1]

        def meta():
            return _sb_tile(qh, km_ref[...], vm_ref[...], tri_m, init, causal_m)[1]

        return lax.cond(i < n_tok_tiles, tok, meta)

    o_ref[...] = jnp.where(_half_mask(t, 0), head(0), head(1)).astype(o_ref.dtype)


def _sm_tile(qh, kb, vb, carry, mask):
    m, l, acc = carry
    s = _qk(qh, kb)
    if mask is not None:
        s = jnp.where(mask, s, NEG_BIG)
    m_new = jnp.maximum(m, jnp.max(s, axis=-1, keepdims=True))
    alpha = jnp.exp(m - m_new)
    p = jnp.exp(s - m_new)
    l = alpha * l + jnp.sum(p, axis=-1, keepdims=True)
    acc = alpha * acc + jnp.dot(p.astype(BF16), vb, preferred_element_type=F32)
    return m_new, l, acc


def _softmax_attend(i, n_tok_tiles, qh, k_at, v_at, km, vm):
    t = ATT_TILE
    chunk_ok = (_iota2((t, t), 1) // CHUNK) <= (_iota2((t, t), 0) // CHUNK)
    init = (jnp.full((t, 1), NEG_BIG, F32), jnp.zeros((t, 1), F32), jnp.zeros((t, LANES), F32))

    def tok():
        start = pl.multiple_of(i * t, t)
        carry = _sm_tile(qh, k_at(start), v_at(start), init, chunk_ok)

        def body(s, c):
            st = pl.multiple_of(s * t, t)
            return _sm_tile(qh, k_at(st), v_at(st), c, None)

        carry = lax.fori_loop(0, i, body, carry)
        return _sm_tile(qh, km, vm, carry, None)

    def meta():
        return _sm_tile(qh, km, vm, init, None)

    _, l, acc = lax.cond(i < n_tok_tiles, tok, meta)
    return acc, l


def _mla_kernel(q_ref, k_ref, v_ref, km_ref, vm_ref, o_ref, *, n_tok_tiles):
    i = pl.program_id(2)
    t = ATT_TILE
    outs = []
    for h in range(2):
        sl = slice(h * LANES, (h + 1) * LANES)
        acc, l = _softmax_attend(
            i, n_tok_tiles, q_ref[:, sl],
            lambda st, sl=sl: k_ref[pl.ds(st, t), sl], lambda st: v_ref[pl.ds(st, t), :],
            km_ref[:, sl], vm_ref[...])
        outs.append(acc * (1.0 / l))
    o_ref[...] = jnp.where(_half_mask(t, 0), outs[0], outs[1]).astype(o_ref.dtype)


def _diff_kernel(q_ref, k_ref, v_ref, km_ref, vm_ref, lam_ref, g_ref, o_ref, *, n_tok_tiles, lam_init):
    i = pl.program_id(2)
    t = ATT_TILE
    q = q_ref[...] * 0.125
    zero = jnp.zeros_like(q)
    outs = []
    for m in range(2):
        qh = jnp.where(_half_mask(t, m), q, zero)
        acc, l = _softmax_attend(
            i, n_tok_tiles, qh,
            lambda st: k_ref[pl.ds(st, t), :], lambda st: v_ref[pl.ds(st, t), :],
            km_ref[...], vm_ref[...])
        outs.append(acc * (1.0 / l))
    lam4 = lam_ref[...]
    lam = (jnp.exp(jnp.sum(lam4[0:1] * lam4[1:2], axis=-1, keepdims=True))
           - jnp.exp(jnp.sum(lam4[2:3] * lam4[3:4], axis=-1, keepdims=True)) + lam_init)
    o = outs[0] - lam * outs[1]
    o_ref[...] = (_rms(o, g_ref[...]) * (1.0 - lam_init)).astype(o_ref.dtype)


def _attn_call(kernel, name, q, qw, qc, k, kw, kc, v, vc, extra, batch, seq, groups):
    rows = q.shape[0]
    t = ATT_TILE
    n_tok_tiles = seq // t
    tok_rows = batch * seq
    meta_q = tok_rows // t
    meta_k = tok_rows // N_META

    def q_map(b, g, i):
        return (jnp.where(i < n_tok_tiles, b * n_tok_tiles + i, meta_q), qc + g)

    def o_map(b, g, i):
        return (jnp.where(i < n_tok_tiles, b * n_tok_tiles + i, meta_q + b), g)

    in_specs = [
        pl.BlockSpec((t, qw), q_map),
        pl.BlockSpec((seq, kw), lambda b, g, i: (b, kc + g)),
        pl.BlockSpec((seq, LANES), lambda b, g, i: (b, vc + g)),
        pl.BlockSpec((N_META, kw), lambda b, g, i: (meta_k, kc + g)),
        pl.BlockSpec((N_META, LANES), lambda b, g, i: (meta_k, vc + g)),
    ] + [pl.BlockSpec(a.shape, lambda b, g, i: (0, 0)) for a in extra]
    return pl.pallas_call(
        functools.partial(kernel, n_tok_tiles=n_tok_tiles),
        grid=(batch, groups, n_tok_tiles + 1),
        in_specs=in_specs,
        out_specs=pl.BlockSpec((t, LANES), o_map),
        out_shape=jax.ShapeDtypeStruct((rows + (batch - 1) * t, WIDTH), BF16),
        compiler_params=_params(("parallel", "parallel", "arbitrary")),
        name=name,
    )(q, k, v, k, v, *extra)


def _perm_w_in(w_in):
    sizes = [512, 512, 512, 512, 384, 256, 32, 512, 512, 512, 512, 512, 3072]
    offs = [0]
    for s in sizes:
        offs.append(offs[-1] + s)
    seg = [w_in[..., offs[n]:offs[n + 1]] for n in range(len(sizes))]
    sb_q, sb_k, sb_v, sb_z, cq, ckv, kr, mz, dq, dk, dv, dz, gate = seg
    pad = lambda n: jnp.zeros(w_in.shape[:-1] + (n,), w_in.dtype)
    cols = [gate, sb_q, sb_k, sb_v, sb_z, dq, dk, dv, dz, mz, ckv, cq, pad(HEAD), kr, pad(LANES - HEAD - MLA_ROPE)]
    return jnp.concatenate(cols, axis=-1).astype(BF16)


def _rope_tables(seq):
    pos = jnp.concatenate([jnp.arange(seq) + N_META, jnp.arange(META_ROWS)]).astype(F32)[:, None]

    def tables(dim, starts):
        inv = ROPE_THETA ** (-jnp.arange(0, dim, 2, dtype=F32) / dim)
        ang = pos * inv[None, :]
        cos, sin = jnp.cos(ang), jnp.sin(ang)
        half = dim // 2
        c = jnp.ones((pos.shape[0], LANES), F32)
        s1 = jnp.zeros((pos.shape[0], LANES), F32)
        s2 = jnp.zeros((pos.shape[0], LANES), F32)
        for st in starts:
            c = c.at[:, st:st + half].set(cos).at[:, st + half:st + dim].set(cos)
            s1 = s1.at[:, st:st + half].set(-sin)
            s2 = s2.at[:, st + half:st + dim].set(sin)
        return [c, s1, s2]

    return tables(MLA_ROPE, [HEAD]) + tables(DIFF_ROT, [0, HEAD])


def _pick(rows, candidates):
    for c in candidates:
        if rows % c == 0:
            return c
    raise ValueError(f"no row tile for {rows}")


def kernel(x, meta_tokens, norm_g, w_in, b_gate, mla_cq_g, mla_ckv_g, mla_w_uq, mla_w_ukv, diff_lambda,
           diff_norm_g, w_o_sb, w_o_mla, w_o_diff, w_out, final_g):
    batch, seq, d = x.shape
    depth = norm_g.shape[0]
    assert d == D_MODEL and seq % ATT_TILE == 0 and meta_tokens.shape == (N_META, D_MODEL)
    tok_rows = batch * seq
    rows = tok_rows + META_ROWS
    tm = ROW_TILE
    tm_proj = _pick(rows, (1280, 1024, 768, 512, 256))

    h = jnp.concatenate([x.reshape(tok_rows, d), meta_tokens.astype(x.dtype),
                         jnp.zeros((META_ROWS - N_META, d), x.dtype)], axis=0)
    w_in_p = _perm_w_in(w_in)
    wuq = jnp.pad(mla_w_uq.reshape(depth, MLA_Q_RANK, MLA_HEADS, HEAD + MLA_ROPE),
                  ((0, 0), (0, 0), (0, 0), (0, LANES - HEAD - MLA_ROPE))).reshape(depth, MLA_Q_RANK, -1).astype(BF16)
    wukv = mla_w_ukv.reshape(depth, MLA_KV_RANK, MLA_HEADS, 2 * HEAD)
    wuk = jnp.pad(wukv[..., :HEAD], ((0, 0), (0, 0), (0, 0), (0, LANES - HEAD))).reshape(depth, MLA_KV_RANK, -1).astype(BF16)
    wuv = wukv[..., HEAD:].reshape(depth, MLA_KV_RANK, -1).astype(BF16)
    tabs = _rope_tables(seq)
    row2 = lambda a: a.reshape(1, -1)

    hn = _norm_call(h, row2(norm_g[0]), tm)
    out = None
    for l in range(depth):
        last = l == depth - 1
        proj = _proj_call(hn, w_in_p[l], tm_proj)
        q_m, k_m, v_m, q_d, k_d = _prep_call(proj, row2(mla_cq_g[l]), row2(mla_ckv_g[l]), wuq[l], wuk[l], wuv[l],
                                             tabs, tm, tok_rows, seq)
        o_sb = _attn_call(_sb_kernel, "sb_attn", proj, LANES, C_SBQ // LANES, proj, LANES, C_SBK // LANES,
                          proj, C_SBV // LANES, [], batch, seq, WIDTH // LANES)
        o_mla = _attn_call(_mla_kernel, "mla_attn", q_m, 2 * LANES, 0, k_m, 2 * LANES, 0, v_m, 0, [],
                           batch, seq, WIDTH // LANES)
        lam_init = 0.8 - 0.6 * math.exp(-0.3 * l)
        o_diff = _attn_call(functools.partial(_diff_kernel, lam_init=lam_init), "diff_attn",
                            q_d, LANES, 0, k_d, LANES, 0, proj, C_DV // LANES,
                            [diff_lambda[l].astype(F32), row2(diff_norm_g[l])], batch, seq, WIDTH // LANES)
        g_next = row2(final_g if last else norm_g[l + 1])
        res = _merge_call(o_sb, o_mla, o_diff, proj, h, row2(b_gate[l]), w_o_sb[l].astype(BF16),
                          w_o_mla[l].astype(BF16), w_o_diff[l].astype(BF16), w_out[l].astype(BF16), g_next,
                          tm, tok_rows if last else rows, last)
        if last:
            out = res[0]
        else:
            h, hn = res
    return out.reshape(batch, seq, d)
```

```python
import functools
import math

import jax
import jax.numpy as jnp
from jax import lax
from jax.experimental import pallas as pl
from jax.experimental.pallas import tpu as pltpu

F32 = jnp.float32
BF16 = jnp.bfloat16

D_MODEL = 1024
CHUNK = 64
N_META = 16
ROPE_THETA = 500000.0
EPS = 1e-6
LANES = 128
HEAD = 64
WIDTH = 512
MLA_HEADS = 8
MLA_ROPE = 32
MLA_Q_RANK = 384
MLA_KV_RANK = 256
DIFF_ROT = 16

META_ROWS = 256
ROW_TILE = 256
ATT_TILE = 256
NEG_BIG = -1e30
VMEM_LIMIT = 56 * 1024 * 1024

C_GATE = 0
C_SBQ = 3072
C_SBK = 3584
C_SBV = 4096
C_SBZ = 4608
C_DQ = 5120
C_DK = 5632
C_DV = 6144
C_DZ = 6656
C_MZ = 7168
C_LAT = 7680
LAT_W = 768
N_PROJ = 8448
PROJ_TN = 768


def _params(sem):
    return pltpu.CompilerParams(dimension_semantics=sem, vmem_limit_bytes=VMEM_LIMIT)


def _rms(x32, g):
    ms = jnp.mean(x32 * x32, axis=-1, keepdims=True)
    return x32 * lax.rsqrt(ms + EPS) * g


def _norm_kernel(h_ref, g_ref, o_ref):
    o_ref[...] = _rms(h_ref[...], g_ref[...]).astype(o_ref.dtype)


def _norm_call(h, g, tm):
    rows = h.shape[0]
    return pl.pallas_call(
        _norm_kernel,
        grid=(rows // tm,),
        in_specs=[pl.BlockSpec((tm, D_MODEL), lambda i: (i, 0)),
                  pl.BlockSpec((1, D_MODEL), lambda i: (0, 0))],
        out_specs=pl.BlockSpec((tm, D_MODEL), lambda i: (i, 0)),
        out_shape=jax.ShapeDtypeStruct((rows, D_MODEL), BF16),
        compiler_params=_params(("parallel",)),
        name="norm_in",
    )(h, g)


def _proj_kernel(a_ref, w_ref, o_ref):
    o_ref[...] = jnp.dot(a_ref[...], w_ref[...], preferred_element_type=F32).astype(o_ref.dtype)


def _proj_call(hn, w, tm):
    rows = hn.shape[0]
    return pl.pallas_call(
        _proj_kernel,
        grid=(rows // tm, N_PROJ // PROJ_TN),
        in_specs=[pl.BlockSpec((tm, D_MODEL), lambda i, j: (i, 0)),
                  pl.BlockSpec((D_MODEL, PROJ_TN), lambda i, j: (0, j))],
        out_specs=pl.BlockSpec((tm, PROJ_TN), lambda i, j: (i, j)),
        out_shape=jax.ShapeDtypeStruct((rows, N_PROJ), BF16),
        compiler_params=_params(("parallel", "arbitrary")),
        name="in_proj",
    )(hn, w)


def _rot(x, c, s1, s2, shift):
    return x * c + pltpu.roll(x, LANES - shift, 1) * s1 + pltpu.roll(x, shift, 1) * s2


def _prep_kernel(lat_ref, dq_ref, dk_ref, gq_ref, gkv_ref, wuq_ref, wuk_ref, wuv_ref,
                 cm_ref, s1m_ref, s2m_ref, cd_ref, s1d_ref, s2d_ref,
                 qm_ref, km_ref, vm_ref, dqo_ref, dko_ref):
    lat = lat_ref[...].astype(F32)
    ckv = lat[:, :MLA_KV_RANK]
    cq = lat[:, MLA_KV_RANK:MLA_KV_RANK + MLA_Q_RANK]
    kr = lat[:, MLA_KV_RANK + MLA_Q_RANK:]
    ncq = _rms(cq, gq_ref[...]).astype(BF16)
    nckv = _rms(ckv, gkv_ref[...]).astype(BF16)
    cm, s1m, s2m = cm_ref[...], s1m_ref[...], s2m_ref[...]
    scale_b = 1.0 / math.sqrt(HEAD + MLA_ROPE)
    k_rope = _rot(kr, cm, s1m, s2m, MLA_ROPE // 2)
    vm_ref[...] = jnp.dot(nckv, wuv_ref[...], preferred_element_type=F32).astype(BF16)
    for h in range(MLA_HEADS):
        sl = slice(h * LANES, (h + 1) * LANES)
        qf = jnp.dot(ncq, wuq_ref[:, sl], preferred_element_type=F32)
        qm_ref[:, sl] = (_rot(qf, cm, s1m, s2m, MLA_ROPE // 2) * scale_b).astype(BF16)
        kf = jnp.dot(nckv, wuk_ref[:, sl], preferred_element_type=F32)
        km_ref[:, sl] = (kf + k_rope).astype(BF16)
    cd, s1d, s2d = cd_ref[...], s1d_ref[...], s2d_ref[...]
    for h in range(WIDTH // LANES):
        sl = slice(h * LANES, (h + 1) * LANES)
        dqo_ref[:, sl] = _rot(dq_ref[:, sl].astype(F32), cd, s1d, s2d, DIFF_ROT // 2).astype(BF16)
        dko_ref[:, sl] = _rot(dk_ref[:, sl].astype(F32), cd, s1d, s2d, DIFF_ROT // 2).astype(BF16)


def _prep_call(proj, gq, gkv, wuq, wuk, wuv, tabs, tm, tok_rows, seq):
    rows = proj.shape[0]
    n_tok_tiles = tok_rows // tm
    per_seq = seq // tm

    def tab_map(i):
        return (jnp.where(i < n_tok_tiles, i % per_seq, per_seq), 0)

    row = lambda w, c: pl.BlockSpec((tm, w), lambda i: (i, c))
    full = lambda a: pl.BlockSpec(a.shape, lambda i: (0, 0))
    tab = pl.BlockSpec((tm, LANES), tab_map)
    out = lambda w: jax.ShapeDtypeStruct((rows, w), BF16)
    return pl.pallas_call(
        _prep_kernel,
        grid=(rows // tm,),
        in_specs=[row(LAT_W, C_LAT // LAT_W), row(WIDTH, C_DQ // WIDTH), row(WIDTH, C_DK // WIDTH),
                  full(gq), full(gkv), full(wuq), full(wuk), full(wuv)] + [tab] * 6,
        out_specs=[row(MLA_HEADS * LANES, 0), row(MLA_HEADS * LANES, 0), row(WIDTH, 0),
                   row(WIDTH, 0), row(WIDTH, 0)],
        out_shape=[out(MLA_HEADS * LANES), out(MLA_HEADS * LANES), out(WIDTH), out(WIDTH), out(WIDTH)],
        compiler_params=_params(("parallel",)),
        name="prep",
    )(proj, proj, proj, gq, gkv, wuq, wuk, wuv, *tabs)


def _sigmoid(x):
    return 1.0 / (1.0 + jnp.exp(-x))


def _merge_kernel(osb_ref, omla_ref, odiff_ref, zsb_ref, zmla_ref, zdiff_ref, g0_ref, g1_ref, g2_ref,
                  h_ref, bg_ref, wsb_ref, wmla_ref, wdiff_ref, wout_ref, gn_ref, *out_refs, last):
    def branch(o_ref, z_ref, w_ref):
        z = z_ref[...].astype(F32)
        a = o_ref[...].astype(F32) * (z * _sigmoid(z))
        return jnp.dot(a.astype(BF16), w_ref[...], preferred_element_type=F32)

    bg = bg_ref[...]
    merged = _sigmoid(g0_ref[...].astype(F32) + bg[:, :D_MODEL]) * branch(osb_ref, zsb_ref, wsb_ref)
    merged += _sigmoid(g1_ref[...].astype(F32) + bg[:, D_MODEL:2 * D_MODEL]) * branch(omla_ref, zmla_ref, wmla_ref)
    merged += _sigmoid(g2_ref[...].astype(F32) + bg[:, 2 * D_MODEL:]) * branch(odiff_ref, zdiff_ref, wdiff_ref)
    h_new = h_ref[...] + jnp.dot(merged.astype(BF16), wout_ref[...], preferred_element_type=F32)
    normed = _rms(h_new, gn_ref[...])
    if last:
        out_refs[0][...] = normed
    else:
        out_refs[0][...] = h_new
        out_refs[1][...] = normed.astype(BF16)


def _merge_call(o_sb, o_mla, o_diff, proj, h, bg, w_sb, w_mla, w_diff, w_out, g_next, tm, out_rows, last):
    row = lambda w, c: pl.BlockSpec((tm, w), lambda i: (i, c))
    full = lambda a: pl.BlockSpec(a.shape, lambda i: (0, 0))
    if last:
        out_specs = [row(D_MODEL, 0)]
        out_shape = [jax.ShapeDtypeStruct((out_rows, D_MODEL), F32)]
    else:
        out_specs = [row(D_MODEL, 0), row(D_MODEL, 0)]
        out_shape = [jax.ShapeDtypeStruct((out_rows, D_MODEL), F32),
                     jax.ShapeDtypeStruct((out_rows, D_MODEL), BF16)]
    return pl.pallas_call(
        functools.partial(_merge_kernel, last=last),
        grid=(out_rows // tm,),
        in_specs=[row(WIDTH, 0), row(WIDTH, 0), row(WIDTH, 0),
                  row(WIDTH, C_SBZ // WIDTH), row(WIDTH, C_MZ // WIDTH), row(WIDTH, C_DZ // WIDTH),
                  row(D_MODEL, 0), row(D_MODEL, 1), row(D_MODEL, 2),
                  row(D_MODEL, 0), full(bg), full(w_sb), full(w_mla), full(w_diff), full(w_out), full(g_next)],
        out_specs=out_specs,
        out_shape=out_shape,
        compiler_params=_params(("parallel",)),
        name="merge_last" if last else "merge",
    )(o_sb, o_mla, o_diff, proj, proj, proj, proj, proj, proj, h, bg, w_sb, w_mla, w_diff, w_out, g_next)


def _qk(q, k):
    return lax.dot_general(q, k, (((1,), (1,)), ((), ())), preferred_element_type=F32)


def _iota2(shape, axis):
    return lax.broadcasted_iota(jnp.int32, shape, axis)


def _half_mask(rows, half):
    lane = _iota2((rows, LANES), 1)
    return (lane >= HEAD) if half else (lane < HEAD)


def _pv(p, v):
    return jnp.dot(p.astype(BF16), v, preferred_element_type=F32)


def _sb_suffix(qs, kb, tri, mask):
    zs = [_qk(qh, kb) for qh in qs]
    lks = [jnp.minimum(-z, 0.0) - jnp.log(1.0 + jnp.exp(-jnp.abs(z))) for z in zs]
    if mask is not None:
        lks = [jnp.where(mask, lk, 0.0) for lk in lks]
    his = [lk.astype(BF16) for lk in lks]
    los = [(lk - hi.astype(F32)).astype(BF16) for lk, hi in zip(lks, his)]
    cums = [jnp.dot(hi, tri, preferred_element_type=F32) + jnp.dot(lo, tri, preferred_element_type=F32)
            for hi, lo in zip(his, los)]
    return [(z + cum, cum[:, 0:1]) for z, cum in zip(zs, cums)]


def _incl_tri(n):
    return (_iota2((n, n), 0) >= _iota2((n, n), 1)).astype(BF16)


def _sb_kernel(q_ref, k_ref, v_ref, km_ref, vm_ref, o_ref, *, n_tok_tiles):
    i = pl.program_id(2)
    t = ATT_TILE
    q = q_ref[...] * 0.125
    tri = _incl_tri(t)
    tri_m = _incl_tri(N_META)
    zero = jnp.zeros_like(q)
    qs = [jnp.where(_half_mask(t, half), q, zero) for half in range(2)]
    km, vm = km_ref[...], vm_ref[...]

    def tok():
        causal = _iota2((t, t), 1) < _iota2((t, t), 0)
        start = pl.multiple_of(i * t, t)
        kd, vd = k_ref[pl.ds(start, t), :], v_ref[pl.ds(start, t), :]
        diag = _sb_suffix(qs, kd, tri, causal)
        meta_part = [logw for logw, _ in _sb_suffix(qs, km, tri_m, None)]
        ws = [jnp.exp(jnp.where(causal, logw, NEG_BIG)) for logw, _ in diag]
        carry = tuple(total for _, total in diag) + tuple(_pv(w, vd) for w in ws)

        def body(s, c):
            st = pl.multiple_of((i - 1 - s) * t, t)
            kb, vb = k_ref[pl.ds(st, t), :], v_ref[pl.ds(st, t), :]
            runs, accs = c[:2], c[2:]
            parts = _sb_suffix(qs, kb, tri, None)
            ws = [jnp.exp(logw + run) for (logw, _), run in zip(parts, runs)]
            new_accs = [acc + _pv(w, vb) for acc, w in zip(accs, ws)]
            return tuple(run + total for run, (_, total) in zip(runs, parts)) + tuple(new_accs)

        c = lax.fori_loop(0, i, body, carry)
        ws = [jnp.exp(part + run) for part, run in zip(meta_part, c[:2])]
        return [acc + _pv(w, vm) for acc, w in zip(c[2:], ws)]

    def meta():
        causal_m = _iota2((t, N_META), 1) < _iota2((t, N_META), 0)
        parts = _sb_suffix(qs, km, tri_m, causal_m)
        return [_pv(jnp.exp(jnp.where(causal_m, logw, NEG_BIG)), vm) for logw, _ in parts]

    o0, o1 = lax.cond(i < n_tok_tiles, tok, meta)
    o_ref[...] = jnp.where(_half_mask(t, 0), o0, o1).astype(o_ref.dtype)


def _row_max(s):
    return jnp.max(s, axis=-1, keepdims=True)


def _row_sum(p):
    return jnp.sum(p, axis=-1, keepdims=True)


def _softmax_attend(i, n_tok_tiles, qs, k_ats, v_at, kms, vm):
    t = ATT_TILE
    n_s = len(qs)

    def tok():
        chunk_ok = (_iota2((t, t), 1) // CHUNK) <= (_iota2((t, t), 0) // CHUNK)
        start = pl.multiple_of(i * t, t)
        vd = v_at(start)
        sds = [jnp.where(chunk_ok, _qk(q, k_at(start)), NEG_BIG) for q, k_at in zip(qs, k_ats)]
        sms = [_qk(q, km) for q, km in zip(qs, kms)]
        ms = [jnp.maximum(_row_max(sd), _row_max(sm)) for sd, sm in zip(sds, sms)]
        pds = [jnp.exp(sd - m) for sd, m in zip(sds, ms)]
        pms = [jnp.exp(sm - m) for sm, m in zip(sms, ms)]
        ls = [_row_sum(pd) + _row_sum(pm) for pd, pm in zip(pds, pms)]
        accs = [_pv(pd, vd) + _pv(pm, vm) for pd, pm in zip(pds, pms)]

        def body(j, c):
            st = pl.multiple_of(j * t, t)
            vb = v_at(st)
            ms, ls, accs = c[:n_s], c[n_s:2 * n_s], c[2 * n_s:]
            ss = [_qk(q, k_at(st)) for q, k_at in zip(qs, k_ats)]
            m_news = [jnp.maximum(m, _row_max(s)) for m, s in zip(ms, ss)]
            alphas = [jnp.exp(m - m_new) for m, m_new in zip(ms, m_news)]
            ps = [jnp.exp(s - m_new) for s, m_new in zip(ss, m_news)]
            l_news = [alpha * l + _row_sum(p) for alpha, l, p in zip(alphas, ls, ps)]
            acc_news = [alpha * acc + _pv(p, vb) for alpha, acc, p in zip(alphas, accs, ps)]
            return tuple(m_news) + tuple(l_news) + tuple(acc_news)

        c = lax.fori_loop(0, i, body, tuple(ms) + tuple(ls) + tuple(accs))
        return [acc * (1.0 / l) for l, acc in zip(c[n_s:2 * n_s], c[2 * n_s:])]

    def meta():
        sms = [_qk(q, km) for q, km in zip(qs, kms)]
        pms = [jnp.exp(sm - _row_max(sm)) for sm in sms]
        return [_pv(pm, vm) * (1.0 / _row_sum(pm)) for pm in pms]

    return lax.cond(i < n_tok_tiles, tok, meta)


def _mla_kernel(q_ref, k_ref, v_ref, km_ref, vm_ref, o_ref, *, n_tok_tiles):
    i = pl.program_id(2)
    t = ATT_TILE
    sls = [slice(h * LANES, (h + 1) * LANES) for h in range(2)]
    o0, o1 = _softmax_attend(
        i, n_tok_tiles, [q_ref[:, sl] for sl in sls],
        [lambda st, sl=sl: k_ref[pl.ds(st, t), sl] for sl in sls], lambda st: v_ref[pl.ds(st, t), :],
        [km_ref[:, sl] for sl in sls], vm_ref[...])
    o_ref[...] = jnp.where(_half_mask(t, 0), o0, o1).astype(o_ref.dtype)


def _diff_kernel(q_ref, k_ref, v_ref, km_ref, vm_ref, lam_ref, g_ref, o_ref, *, n_tok_tiles, lam_init):
    i = pl.program_id(2)
    t = ATT_TILE
    q = q_ref[...] * 0.125
    zero = jnp.zeros_like(q)
    k_at = lambda st: k_ref[pl.ds(st, t), :]
    o0, o1 = _softmax_attend(
        i, n_tok_tiles, [jnp.where(_half_mask(t, m), q, zero) for m in range(2)],
        [k_at, k_at], lambda st: v_ref[pl.ds(st, t), :], [km_ref[...]] * 2, vm_ref[...])
    lam4 = lam_ref[...]
    lam = (jnp.exp(jnp.sum(lam4[0:1] * lam4[1:2], axis=-1, keepdims=True))
           - jnp.exp(jnp.sum(lam4[2:3] * lam4[3:4], axis=-1, keepdims=True)) + lam_init)
    o = o0 - lam * o1
    o_ref[...] = (_rms(o, g_ref[...]) * (1.0 - lam_init)).astype(o_ref.dtype)


def _attn_call(kernel, name, q, qw, qc, k, kw, kc, v, vc, extra, batch, seq, groups):
    rows = q.shape[0]
    t = ATT_TILE
    n_tok_tiles = seq // t
    tok_rows = batch * seq
    meta_q = tok_rows // t
    meta_k = tok_rows // N_META

    def q_map(b, g, i):
        return (jnp.where(i < n_tok_tiles, b * n_tok_tiles + i, meta_q), qc + g)

    def o_map(b, g, i):
        return (jnp.where(i < n_tok_tiles, b * n_tok_tiles + i, meta_q + b), g)

    in_specs = [
        pl.BlockSpec((t, qw), q_map),
        pl.BlockSpec((seq, kw), lambda b, g, i: (b, kc + g)),
        pl.BlockSpec((seq, LANES), lambda b, g, i: (b, vc + g)),
        pl.BlockSpec((N_META, kw), lambda b, g, i: (meta_k, kc + g)),
        pl.BlockSpec((N_META, LANES), lambda b, g, i: (meta_k, vc + g)),
    ] + [pl.BlockSpec(a.shape, lambda b, g, i: (0, 0)) for a in extra]
    return pl.pallas_call(
        functools.partial(kernel, n_tok_tiles=n_tok_tiles),
        grid=(batch, groups, n_tok_tiles + 1),
        in_specs=in_specs,
        out_specs=pl.BlockSpec((t, LANES), o_map),
        out_shape=jax.ShapeDtypeStruct((rows + (batch - 1) * t, WIDTH), BF16),
        compiler_params=_params(("parallel", "parallel", "arbitrary")),
        name=name,
    )(q, k, v, k, v, *extra)


def _perm_w_in(w_in):
    sizes = [512, 512, 512, 512, 384, 256, 32, 512, 512, 512, 512, 512, 3072]
    offs = [0]
    for s in sizes:
        offs.append(offs[-1] + s)
    seg = [w_in[..., offs[n]:offs[n + 1]] for n in range(len(sizes))]
    sb_q, sb_k, sb_v, sb_z, cq, ckv, kr, mz, dq, dk, dv, dz, gate = seg
    pad = lambda n: jnp.zeros(w_in.shape[:-1] + (n,), w_in.dtype)
    cols = [gate, sb_q, sb_k, sb_v, sb_z, dq, dk, dv, dz, mz, ckv, cq, pad(HEAD), kr, pad(LANES - HEAD - MLA_ROPE)]
    return jnp.concatenate(cols, axis=-1).astype(BF16)


def _rope_tables(seq):
    pos = jnp.concatenate([jnp.arange(seq) + N_META, jnp.arange(META_ROWS)]).astype(F32)[:, None]

    def tables(dim, starts):
        inv = ROPE_THETA ** (-jnp.arange(0, dim, 2, dtype=F32) / dim)
        ang = pos * inv[None, :]
        cos, sin = jnp.cos(ang), jnp.sin(ang)
        half = dim // 2
        c = jnp.ones((pos.shape[0], LANES), F32)
        s1 = jnp.zeros((pos.shape[0], LANES), F32)
        s2 = jnp.zeros((pos.shape[0], LANES), F32)
        for st in starts:
            c = c.at[:, st:st + half].set(cos).at[:, st + half:st + dim].set(cos)
            s1 = s1.at[:, st:st + half].set(-sin)
            s2 = s2.at[:, st + half:st + dim].set(sin)
        return [c, s1, s2]

    return tables(MLA_ROPE, [HEAD]) + tables(DIFF_ROT, [0, HEAD])


def _pick(rows, candidates):
    for c in candidates:
        if rows % c == 0:
            return c
    raise ValueError(f"no row tile for {rows}")


def kernel(x, meta_tokens, norm_g, w_in, b_gate, mla_cq_g, mla_ckv_g, mla_w_uq, mla_w_ukv, diff_lambda,
           diff_norm_g, w_o_sb, w_o_mla, w_o_diff, w_out, final_g):
    batch, seq, d = x.shape
    depth = norm_g.shape[0]
    assert d == D_MODEL and seq % ATT_TILE == 0 and meta_tokens.shape == (N_META, D_MODEL)
    tok_rows = batch * seq
    rows = tok_rows + META_ROWS
    tm = ROW_TILE
    tm_proj = _pick(rows, (1280, 1024, 768, 512, 256))

    h = jnp.concatenate([x.reshape(tok_rows, d), meta_tokens.astype(x.dtype),
                         jnp.zeros((META_ROWS - N_META, d), x.dtype)], axis=0)
    w_in_p = _perm_w_in(w_in)
    wuq = jnp.pad(mla_w_uq.reshape(depth, MLA_Q_RANK, MLA_HEADS, HEAD + MLA_ROPE),
                  ((0, 0), (0, 0), (0, 0), (0, LANES - HEAD - MLA_ROPE))).reshape(depth, MLA_Q_RANK, -1).astype(BF16)
    wukv = mla_w_ukv.reshape(depth, MLA_KV_RANK, MLA_HEADS, 2 * HEAD)
    wuk = jnp.pad(wukv[..., :HEAD], ((0, 0), (0, 0), (0, 0), (0, LANES - HEAD))).reshape(depth, MLA_KV_RANK, -1).astype(BF16)
    wuv = wukv[..., HEAD:].reshape(depth, MLA_KV_RANK, -1).astype(BF16)
    tabs = _rope_tables(seq)
    row2 = lambda a: a.reshape(1, -1)

    hn = _norm_call(h, row2(norm_g[0]), tm)
    out = None
    for l in range(depth):
        last = l == depth - 1
        proj = _proj_call(hn, w_in_p[l], tm_proj)
        q_m, k_m, v_m, q_d, k_d = _prep_call(proj, row2(mla_cq_g[l]), row2(mla_ckv_g[l]), wuq[l], wuk[l], wuv[l],
                                             tabs, tm, tok_rows, seq)
        o_sb = _attn_call(_sb_kernel, "sb_attn", proj, LANES, C_SBQ // LANES, proj, LANES, C_SBK // LANES,
                          proj, C_SBV // LANES, [], batch, seq, WIDTH // LANES)
        o_mla = _attn_call(_mla_kernel, "mla_attn", q_m, 2 * LANES, 0, k_m, 2 * LANES, 0, v_m, 0, [],
                           batch, seq, WIDTH // LANES)
        lam_init = 0.8 - 0.6 * math.exp(-0.3 * l)
        o_diff = _attn_call(functools.partial(_diff_kernel, lam_init=lam_init), "diff_attn",
                            q_d, LANES, 0, k_d, LANES, 0, proj, C_DV // LANES,
                            [diff_lambda[l].astype(F32), row2(diff_norm_g[l])], batch, seq, WIDTH // LANES)
        g_next = row2(final_g if last else norm_g[l + 1])
        res = _merge_call(o_sb, o_mla, o_diff, proj, h, row2(b_gate[l]), w_o_sb[l].astype(BF16),
                          w_o_mla[l].astype(BF16), w_o_diff[l].astype(BF16), w_out[l].astype(BF16), g_next,
                          tm, tok_rows if last else rows, last)
        if last:
            out = res[0]
        else:
            h, hn = res
    return out.reshape(batch, seq, d)
```

```python
import functools
import math

import jax
import jax.numpy as jnp
from jax import lax
from jax.experimental import pallas as pl
from jax.experimental.pallas import tpu as pltpu

F32 = jnp.float32
BF16 = jnp.bfloat16

D_MODEL = 1024
CHUNK = 64
N_META = 16
ROPE_THETA = 500000.0
EPS = 1e-6
LANES = 128
HEAD = 64
WIDTH = 512
MLA_HEADS = 8
MLA_ROPE = 32
MLA_Q_RANK = 384
MLA_KV_RANK = 256
DIFF_ROT = 16

META_ROWS = 256
ROW_TILE = 256
ATT_TILE = 256
ROW_CHUNK = 128
ATT_GROUPS = 2
NEG_BIG = -1e30
VMEM_LIMIT = 56 * 1024 * 1024

C_GATE = 0
C_SBQ = 3072
C_SBK = 3584
C_SBV = 4096
C_SBZ = 4608
C_DQ = 5120
C_DK = 5632
C_DV = 6144
C_DZ = 6656
C_MZ = 7168
C_LAT = 7680
LAT_W = 768
N_PROJ = 8448
PROJ_TN = 768


def _params(sem):
    return pltpu.CompilerParams(dimension_semantics=sem, vmem_limit_bytes=VMEM_LIMIT)


def _rms(x32, g):
    ms = jnp.mean(x32 * x32, axis=-1, keepdims=True)
    return x32 * lax.rsqrt(ms + EPS) * g


def _norm_kernel(h_ref, g_ref, o_ref):
    o_ref[...] = _rms(h_ref[...], g_ref[...]).astype(o_ref.dtype)


def _norm_call(h, g, tm):
    rows = h.shape[0]
    return pl.pallas_call(
        _norm_kernel,
        grid=(rows // tm,),
        in_specs=[pl.BlockSpec((tm, D_MODEL), lambda i: (i, 0)),
                  pl.BlockSpec((1, D_MODEL), lambda i: (0, 0))],
        out_specs=pl.BlockSpec((tm, D_MODEL), lambda i: (i, 0)),
        out_shape=jax.ShapeDtypeStruct((rows, D_MODEL), BF16),
        compiler_params=_params(("parallel",)),
        name="norm_in",
    )(h, g)


def _proj_kernel(a_ref, w_ref, o_ref):
    o_ref[...] = jnp.dot(a_ref[...], w_ref[...], preferred_element_type=F32).astype(o_ref.dtype)


def _proj_call(hn, w, tm):
    rows = hn.shape[0]
    return pl.pallas_call(
        _proj_kernel,
        grid=(rows // tm, N_PROJ // PROJ_TN),
        in_specs=[pl.BlockSpec((tm, D_MODEL), lambda i, j: (i, 0)),
                  pl.BlockSpec((D_MODEL, PROJ_TN), lambda i, j: (0, j))],
        out_specs=pl.BlockSpec((tm, PROJ_TN), lambda i, j: (i, j)),
        out_shape=jax.ShapeDtypeStruct((rows, N_PROJ), BF16),
        compiler_params=_params(("parallel", "arbitrary")),
        name="in_proj",
    )(hn, w)


def _rot(x, c, s1, s2, shift):
    return x * c + pltpu.roll(x, LANES - shift, 1) * s1 + pltpu.roll(x, shift, 1) * s2


def _prep_kernel(lat_ref, dq_ref, dk_ref, gq_ref, gkv_ref, wuq_ref, wuk_ref, wuv_ref,
                 cm_ref, s1m_ref, s2m_ref, cd_ref, s1d_ref, s2d_ref,
                 qm_ref, km_ref, vm_ref, dqo_ref, dko_ref):
    lat = lat_ref[...].astype(F32)
    ckv = lat[:, :MLA_KV_RANK]
    cq = lat[:, MLA_KV_RANK:MLA_KV_RANK + MLA_Q_RANK]
    kr = lat[:, MLA_KV_RANK + MLA_Q_RANK:]
    ncq = _rms(cq, gq_ref[...]).astype(BF16)
    nckv = _rms(ckv, gkv_ref[...]).astype(BF16)
    cm, s1m, s2m = cm_ref[...], s1m_ref[...], s2m_ref[...]
    scale_b = 1.0 / math.sqrt(HEAD + MLA_ROPE)
    k_rope = _rot(kr, cm, s1m, s2m, MLA_ROPE // 2)
    vm_ref[...] = jnp.dot(nckv, wuv_ref[...], preferred_element_type=F32).astype(BF16)
    for h in range(MLA_HEADS):
        sl = slice(h * LANES, (h + 1) * LANES)
        qf = jnp.dot(ncq, wuq_ref[:, sl], preferred_element_type=F32)
        qm_ref[:, sl] = (_rot(qf, cm, s1m, s2m, MLA_ROPE // 2) * scale_b).astype(BF16)
        kf = jnp.dot(nckv, wuk_ref[:, sl], preferred_element_type=F32)
        km_ref[:, sl] = (kf + k_rope).astype(BF16)
    cd, s1d, s2d = cd_ref[...], s1d_ref[...], s2d_ref[...]
    for h in range(WIDTH // LANES):
        sl = slice(h * LANES, (h + 1) * LANES)
        dqo_ref[:, sl] = _rot(dq_ref[:, sl].astype(F32), cd, s1d, s2d, DIFF_ROT // 2).astype(BF16)
        dko_ref[:, sl] = _rot(dk_ref[:, sl].astype(F32), cd, s1d, s2d, DIFF_ROT // 2).astype(BF16)


def _prep_call(proj, gq, gkv, wuq, wuk, wuv, tabs, tm, tok_rows, seq):
    rows = proj.shape[0]
    n_tok_tiles = tok_rows // tm
    per_seq = seq // tm

    def tab_map(i):
        return (jnp.where(i < n_tok_tiles, i % per_seq, per_seq), 0)

    row = lambda w, c: pl.BlockSpec((tm, w), lambda i: (i, c))
    full = lambda a: pl.BlockSpec(a.shape, lambda i: (0, 0))
    tab = pl.BlockSpec((tm, LANES), tab_map)
    out = lambda w: jax.ShapeDtypeStruct((rows, w), BF16)
    return pl.pallas_call(
        _prep_kernel,
        grid=(rows // tm,),
        in_specs=[row(LAT_W, C_LAT // LAT_W), row(WIDTH, C_DQ // WIDTH), row(WIDTH, C_DK // WIDTH),
                  full(gq), full(gkv), full(wuq), full(wuk), full(wuv)] + [tab] * 6,
        out_specs=[row(MLA_HEADS * LANES, 0), row(MLA_HEADS * LANES, 0), row(WIDTH, 0),
                   row(WIDTH, 0), row(WIDTH, 0)],
        out_shape=[out(MLA_HEADS * LANES), out(MLA_HEADS * LANES), out(WIDTH), out(WIDTH), out(WIDTH)],
        compiler_params=_params(("parallel",)),
        name="prep",
    )(proj, proj, proj, gq, gkv, wuq, wuk, wuv, *tabs)


def _sigmoid(x):
    return 1.0 / (1.0 + jnp.exp(-x))


def _merge_kernel(osb_ref, omla_ref, odiff_ref, zsb_ref, zmla_ref, zdiff_ref, g0_ref, g1_ref, g2_ref,
                  h_ref, bg_ref, wsb_ref, wmla_ref, wdiff_ref, wout_ref, gn_ref, *out_refs, last):
    def branch(o_ref, z_ref, w_ref):
        z = z_ref[...].astype(F32)
        a = o_ref[...].astype(F32) * (z * _sigmoid(z))
        return jnp.dot(a.astype(BF16), w_ref[...], preferred_element_type=F32)

    bg = bg_ref[...]
    merged = _sigmoid(g0_ref[...].astype(F32) + bg[:, :D_MODEL]) * branch(osb_ref, zsb_ref, wsb_ref)
    merged += _sigmoid(g1_ref[...].astype(F32) + bg[:, D_MODEL:2 * D_MODEL]) * branch(omla_ref, zmla_ref, wmla_ref)
    merged += _sigmoid(g2_ref[...].astype(F32) + bg[:, 2 * D_MODEL:]) * branch(odiff_ref, zdiff_ref, wdiff_ref)
    h_new = h_ref[...] + jnp.dot(merged.astype(BF16), wout_ref[...], preferred_element_type=F32)
    normed = _rms(h_new, gn_ref[...])
    if last:
        out_refs[0][...] = normed
    else:
        out_refs[0][...] = h_new
        out_refs[1][...] = normed.astype(BF16)


def _merge_call(o_sb, o_mla, o_diff, proj, h, bg, w_sb, w_mla, w_diff, w_out, g_next, tm, out_rows, last):
    row = lambda w, c: pl.BlockSpec((tm, w), lambda i: (i, c))
    full = lambda a: pl.BlockSpec(a.shape, lambda i: (0, 0))
    if last:
        out_specs = [row(D_MODEL, 0)]
        out_shape = [jax.ShapeDtypeStruct((out_rows, D_MODEL), F32)]
    else:
        out_specs = [row(D_MODEL, 0), row(D_MODEL, 0)]
        out_shape = [jax.ShapeDtypeStruct((out_rows, D_MODEL), F32),
                     jax.ShapeDtypeStruct((out_rows, D_MODEL), BF16)]
    return pl.pallas_call(
        functools.partial(_merge_kernel, last=last),
        grid=(out_rows // tm,),
        in_specs=[row(WIDTH, 0), row(WIDTH, 0), row(WIDTH, 0),
                  row(WIDTH, C_SBZ // WIDTH), row(WIDTH, C_MZ // WIDTH), row(WIDTH, C_DZ // WIDTH),
                  row(D_MODEL, 0), row(D_MODEL, 1), row(D_MODEL, 2),
                  row(D_MODEL, 0), full(bg), full(w_sb), full(w_mla), full(w_diff), full(w_out), full(g_next)],
        out_specs=out_specs,
        out_shape=out_shape,
        compiler_params=_params(("parallel",)),
        name="merge_last" if last else "merge",
    )(o_sb, o_mla, o_diff, proj, proj, proj, proj, proj, proj, h, bg, w_sb, w_mla, w_diff, w_out, g_next)


def _qk(q, k):
    return lax.dot_general(q, k, (((1,), (1,)), ((), ())), preferred_element_type=F32)


def _iota2(shape, axis):
    return lax.broadcasted_iota(jnp.int32, shape, axis)


def _half_mask(rows, half):
    lane = _iota2((rows, LANES), 1)
    return (lane >= HEAD) if half else (lane < HEAD)


def _pv(p, v):
    return jnp.dot(p.astype(BF16), v, preferred_element_type=F32)


def _incl_tri(n):
    return (_iota2((n, n), 0) >= _iota2((n, n), 1)).astype(BF16)


def _sb_logw(z, tri):
    lk = jnp.minimum(-z, 0.0) - jnp.log(1.0 + jnp.exp(-jnp.abs(z)))
    hi = lk.astype(BF16)
    lo = (lk - hi.astype(F32)).astype(BF16)
    cum = jnp.dot(hi, tri, preferred_element_type=F32) + jnp.dot(lo, tri, preferred_element_type=F32)
    return z + cum, cum[:, 0:1]


def _wide(x, width):
    return jnp.concatenate([x] * (width // LANES), axis=-1)


def _row_blocks(t):
    return [pl.ds(r * ROW_CHUNK, ROW_CHUNK) for r in range(t // ROW_CHUNK)]


def _stream_cols(n_streams):
    return [slice((n // 2) * LANES, (n // 2 + 1) * LANES) for n in range(n_streams)]


def _half_queries(q_ref, t):
    qs = []
    for n, c in enumerate(_stream_cols(2 * (q_ref.shape[1] // LANES))):
        qg = q_ref[:, c] * 0.125
        qs.append(jnp.where(_half_mask(t, n % 2), qg, jnp.zeros_like(qg)))
    return qs


def _join_halves(o_ref, outs, t):
    for g in range(len(outs) // 2):
        o_ref[:, g * LANES:(g + 1) * LANES] = jnp.where(_half_mask(t, 0), outs[2 * g], outs[2 * g + 1]).astype(o_ref.dtype)


def _sb_kernel(q_ref, k_ref, v_ref, km_ref, vm_ref, o_ref, z_scr, w_scr, acc_scr, run_scr, mp_scr, *, n_tok_tiles):
    i = pl.program_id(2)
    t = ATT_TILE
    qs = _half_queries(q_ref, t)
    n_s = len(qs)
    cols = _stream_cols(n_s)
    tri = _incl_tri(t)
    tri_m = _incl_tri(N_META)
    kms = [km_ref[:, c] for c in cols]
    vms = [vm_ref[:, c] for c in cols]
    blocks = _row_blocks(t)

    def tiles_at(ref, p):
        rows = pl.ds(pl.multiple_of(jnp.maximum(i - p, 0) * t, t), t)
        per_group = [ref[rows, cols[2 * g]] for g in range(n_s // 2)]
        return [per_group[n // 2] for n in range(n_s)]

    def scores(p):
        return [_qk(qh, kb) for qh, kb in zip(qs, tiles_at(k_ref, p))]

    def fold(vbs):
        loaded = [(n, r, w_scr[n, r, :], acc_scr[n, r, :]) for n in range(n_s) for r in blocks]
        return [(acc_scr, n, r, acc + _pv(jnp.exp(logw), vbs[n])) for n, r, logw, acc in loaded]

    def advance(zs, runs, masks):
        out = []
        for n in range(n_s):
            for b, r in enumerate(blocks):
                z, run = zs[n][b], runs[n][b]
                if masks is not None:
                    z = jnp.where(masks[b], z, NEG_BIG)
                part, total = _sb_logw(z, tri)
                out += [(w_scr, n, r, part + _wide(run, t)), (run_scr, n, r, run + total)]
        return out

    def store(writes):
        for ref, n, r, val in writes:
            ref[n, r, :] = val

    def tok():
        z0, z1 = scores(0), scores(1)
        zm = [_qk(qh, km) for qh, km in zip(qs, kms)]
        causal = [_iota2((ROW_CHUNK, t), 1) < _iota2((ROW_CHUNK, t), 0) + b * ROW_CHUNK for b in range(len(blocks))]
        acc_scr[...] = jnp.zeros_like(acc_scr)
        zero_run = jnp.zeros((ROW_CHUNK, LANES), F32)
        z0_blocks = [[z[b * ROW_CHUNK:(b + 1) * ROW_CHUNK] for b in range(len(blocks))] for z in z0]
        store(advance(z0_blocks, [[zero_run] * len(blocks)] * n_s, causal))
        for n in range(n_s):
            z_scr[n] = z1[n]
            mp_scr[n] = _sb_logw(zm[n], tri_m)[0]

        def body(p, carry):
            z_next = scores(p + 2)
            zs = [[z_scr[n, r, :] for r in blocks] for n in range(n_s)]
            runs = [[run_scr[n, r, :] for r in blocks] for n in range(n_s)]
            store(fold(tiles_at(v_ref, p)) + advance(zs, runs, None))
            for n in range(n_s):
                z_scr[n] = z_next[n]
            return carry

        lax.fori_loop(0, i, body, 0)
        store(fold(tiles_at(v_ref, i)))
        return [acc_scr[n] + _pv(jnp.exp(mp_scr[n] + run_scr[n][:, :N_META]), vms[n]) for n in range(n_s)]

    def meta():
        causal_m = _iota2((t, N_META), 1) < _iota2((t, N_META), 0)
        zm = [_qk(qh, km) for qh, km in zip(qs, kms)]
        return [_pv(jnp.exp(_sb_logw(jnp.where(causal_m, z, NEG_BIG), tri_m)[0]), vm) for z, vm in zip(zm, vms)]

    _join_halves(o_ref, lax.cond(i < n_tok_tiles, tok, meta), t)


def _sb_scratch(n_s):
    t = ATT_TILE
    return [pltpu.VMEM((n_s, t, t), F32), pltpu.VMEM((n_s, t, t), F32), pltpu.VMEM((n_s, t, LANES), F32),
            pltpu.VMEM((n_s, t, LANES), F32), pltpu.VMEM((n_s, t, N_META), F32)]


def _row_max(s):
    return jnp.max(s, axis=-1, keepdims=True)


def _row_sum(p):
    return jnp.sum(p, axis=-1, keepdims=True)


def _softmax_attend(i, n_tok_tiles, qs, k_ats, v_ats, kms, vms, s_scr, acc_scr, m_scr, l_scr):
    t = ATT_TILE
    n_s = len(qs)
    blocks = _row_blocks(t)

    def fold(vbs):
        for n in range(n_s):
            for r in blocks:
                s, m_old, l_old, acc_old = s_scr[n, r, :], m_scr[n, r, :], l_scr[n, r, :], acc_scr[n, r, :]
                m_new = jnp.maximum(m_old, _row_max(s))
                alpha = jnp.exp(m_old - m_new)
                p = jnp.exp(s - _wide(m_new, t))
                m_scr[n, r, :] = m_new
                l_scr[n, r, :] = alpha * l_old + _row_sum(p)
                acc_scr[n, r, :] = alpha * acc_old + _pv(p, vbs[n])

    def scores(j):
        st = pl.multiple_of(j * t, t)
        return [_qk(q, k_at(st)) for q, k_at in zip(qs, k_ats)]

    def values(j):
        st = pl.multiple_of(j * t, t)
        return [v_at(st) for v_at in v_ats]

    def tok():
        sds, s0s = scores(i), scores(0)
        sms = [_qk(q, km) for q, km in zip(qs, kms)]
        vds = values(i)
        for n in range(n_s):
            for b, r in enumerate(blocks):
                rows = slice(b * ROW_CHUNK, (b + 1) * ROW_CHUNK)
                q_chunk = _iota2((ROW_CHUNK, t), 0) // CHUNK + b * (ROW_CHUNK // CHUNK)
                sd = jnp.where(_iota2((ROW_CHUNK, t), 1) // CHUNK <= q_chunk, sds[n][rows], NEG_BIG)
                sm = sms[n][rows]
                m = jnp.maximum(_row_max(sd), _row_max(sm))
                pd, pm = jnp.exp(sd - m), jnp.exp(sm - m)
                m_scr[n, r, :] = jnp.broadcast_to(m, (ROW_CHUNK, LANES))
                l_scr[n, r, :] = jnp.broadcast_to(_row_sum(pd) + _row_sum(pm), (ROW_CHUNK, LANES))
                acc_scr[n, r, :] = _pv(pd, vds[n]) + _pv(pm, vms[n])
        for n in range(n_s):
            s_scr[n] = s0s[n]

        def body(j, carry):
            s_next = scores(jnp.minimum(j + 1, i - 1))
            fold(values(j))
            for n in range(n_s):
                s_scr[n] = s_next[n]
            return carry

        lax.fori_loop(0, i, body, 0)
        return [acc_scr[n] * (1.0 / l_scr[n]) for n in range(n_s)]

    def meta():
        sms = [_qk(q, km) for q, km in zip(qs, kms)]
        pms = [jnp.exp(sm - _row_max(sm)) for sm in sms]
        return [_pv(pm, vm) * (1.0 / _row_sum(pm)) for pm, vm in zip(pms, vms)]

    return lax.cond(i < n_tok_tiles, tok, meta)


def _softmax_scratch(n_s):
    t = ATT_TILE
    return [pltpu.VMEM((n_s, t, t), F32), pltpu.VMEM((n_s, t, LANES), F32), pltpu.VMEM((n_s, t, LANES), F32),
            pltpu.VMEM((n_s, t, LANES), F32)]


def _mla_kernel(q_ref, k_ref, v_ref, km_ref, vm_ref, o_ref, *scratch, n_tok_tiles):
    i = pl.program_id(2)
    t = ATT_TILE
    n_s = q_ref.shape[1] // LANES
    qcols = [slice(n * LANES, (n + 1) * LANES) for n in range(n_s)]
    vcols = _stream_cols(n_s)
    outs = _softmax_attend(
        i, n_tok_tiles, [q_ref[:, c] for c in qcols],
        [lambda st, c=c: k_ref[pl.ds(st, t), c] for c in qcols], [lambda st, c=c: v_ref[pl.ds(st, t), c] for c in vcols],
        [km_ref[:, c] for c in qcols], [vm_ref[:, c] for c in vcols], *scratch)
    _join_halves(o_ref, outs, t)


def _diff_kernel(q_ref, k_ref, v_ref, km_ref, vm_ref, lam_ref, g_ref, o_ref, *scratch, n_tok_tiles, lam_init):
    i = pl.program_id(2)
    t = ATT_TILE
    qs = _half_queries(q_ref, t)
    cols = _stream_cols(len(qs))
    outs = _softmax_attend(
        i, n_tok_tiles, qs,
        [lambda st, c=c: k_ref[pl.ds(st, t), c] for c in cols], [lambda st, c=c: v_ref[pl.ds(st, t), c] for c in cols],
        [km_ref[:, c] for c in cols], [vm_ref[:, c] for c in cols], *scratch)
    lam4 = lam_ref[...]
    lam = (jnp.exp(jnp.sum(lam4[0:1] * lam4[1:2], axis=-1, keepdims=True))
           - jnp.exp(jnp.sum(lam4[2:3] * lam4[3:4], axis=-1, keepdims=True)) + lam_init)
    for g in range(len(qs) // 2):
        o = outs[2 * g] - lam * outs[2 * g + 1]
        o_ref[:, g * LANES:(g + 1) * LANES] = (_rms(o, g_ref[...]) * (1.0 - lam_init)).astype(o_ref.dtype)


def _attn_call(kernel, name, q, qw, qc, k, kw, kc, v, vc, extra, scratch, batch, seq, groups):
    rows = q.shape[0]
    t = ATT_TILE
    n_tok_tiles = seq // t
    tok_rows = batch * seq
    meta_q = tok_rows // t
    meta_k = tok_rows // N_META
    vw = WIDTH // groups

    def q_map(b, g, i):
        return (jnp.where(i < n_tok_tiles, b * n_tok_tiles + i, meta_q), qc + g)

    def o_map(b, g, i):
        return (jnp.where(i < n_tok_tiles, b * n_tok_tiles + i, meta_q + b), g)

    in_specs = [
        pl.BlockSpec((t, qw), q_map),
        pl.BlockSpec((seq, kw), lambda b, g, i: (b, kc + g)),
        pl.BlockSpec((seq, vw), lambda b, g, i: (b, vc + g)),
        pl.BlockSpec((N_META, kw), lambda b, g, i: (meta_k, kc + g)),
        pl.BlockSpec((N_META, vw), lambda b, g, i: (meta_k, vc + g)),
    ] + [pl.BlockSpec(a.shape, lambda b, g, i: (0, 0)) for a in extra]
    return pl.pallas_call(
        functools.partial(kernel, n_tok_tiles=n_tok_tiles),
        grid=(batch, groups, n_tok_tiles + 1),
        in_specs=in_specs,
        out_specs=pl.BlockSpec((t, vw), o_map),
        out_shape=jax.ShapeDtypeStruct((rows + (batch - 1) * t, WIDTH), BF16),
        scratch_shapes=scratch,
        compiler_params=_params(("parallel", "parallel", "arbitrary")),
        name=name,
    )(q, k, v, k, v, *extra)


def _perm_w_in(w_in):
    sizes = [512, 512, 512, 512, 384, 256, 32, 512, 512, 512, 512, 512, 3072]
    offs = [0]
    for s in sizes:
        offs.append(offs[-1] + s)
    seg = [w_in[..., offs[n]:offs[n + 1]] for n in range(len(sizes))]
    sb_q, sb_k, sb_v, sb_z, cq, ckv, kr, mz, dq, dk, dv, dz, gate = seg
    pad = lambda n: jnp.zeros(w_in.shape[:-1] + (n,), w_in.dtype)
    cols = [gate, sb_q, sb_k, sb_v, sb_z, dq, dk, dv, dz, mz, ckv, cq, pad(HEAD), kr, pad(LANES - HEAD - MLA_ROPE)]
    return jnp.concatenate(cols, axis=-1).astype(BF16)


def _rope_tables(seq):
    pos = jnp.concatenate([jnp.arange(seq) + N_META, jnp.arange(META_ROWS)]).astype(F32)[:, None]

    def tables(dim, starts):
        inv = ROPE_THETA ** (-jnp.arange(0, dim, 2, dtype=F32) / dim)
        ang = pos * inv[None, :]
        cos, sin = jnp.cos(ang), jnp.sin(ang)
        half = dim // 2
        c = jnp.ones((pos.shape[0], LANES), F32)
        s1 = jnp.zeros((pos.shape[0], LANES), F32)
        s2 = jnp.zeros((pos.shape[0], LANES), F32)
        for st in starts:
            c = c.at[:, st:st + half].set(cos).at[:, st + half:st + dim].set(cos)
            s1 = s1.at[:, st:st + half].set(-sin)
            s2 = s2.at[:, st + half:st + dim].set(sin)
        return [c, s1, s2]

    return tables(MLA_ROPE, [HEAD]) + tables(DIFF_ROT, [0, HEAD])


def _pick(rows, candidates):
    for c in candidates:
        if rows % c == 0:
            return c
    raise ValueError(f"no row tile for {rows}")


def kernel(x, meta_tokens, norm_g, w_in, b_gate, mla_cq_g, mla_ckv_g, mla_w_uq, mla_w_ukv, diff_lambda,
           diff_norm_g, w_o_sb, w_o_mla, w_o_diff, w_out, final_g):
    batch, seq, d = x.shape
    depth = norm_g.shape[0]
    assert d == D_MODEL and seq % ATT_TILE == 0 and meta_tokens.shape == (N_META, D_MODEL)
    tok_rows = batch * seq
    rows = tok_rows + META_ROWS
    tm = ROW_TILE
    tm_proj = _pick(rows, (1280, 1024, 768, 512, 256))

    h = jnp.concatenate([x.reshape(tok_rows, d), meta_tokens.astype(x.dtype),
                         jnp.zeros((META_ROWS - N_META, d), x.dtype)], axis=0)
    w_in_p = _perm_w_in(w_in)
    wuq = jnp.pad(mla_w_uq.reshape(depth, MLA_Q_RANK, MLA_HEADS, HEAD + MLA_ROPE),
                  ((0, 0), (0, 0), (0, 0), (0, LANES - HEAD - MLA_ROPE))).reshape(depth, MLA_Q_RANK, -1).astype(BF16)
    wukv = mla_w_ukv.reshape(depth, MLA_KV_RANK, MLA_HEADS, 2 * HEAD)
    wuk = jnp.pad(wukv[..., :HEAD], ((0, 0), (0, 0), (0, 0), (0, LANES - HEAD))).reshape(depth, MLA_KV_RANK, -1).astype(BF16)
    wuv = wukv[..., HEAD:].reshape(depth, MLA_KV_RANK, -1).astype(BF16)
    tabs = _rope_tables(seq)
    row2 = lambda a: a.reshape(1, -1)

    hn = _norm_call(h, row2(norm_g[0]), tm)
    out = None
    for l in range(depth):
        last = l == depth - 1
        proj = _proj_call(hn, w_in_p[l], tm_proj)
        q_m, k_m, v_m, q_d, k_d = _prep_call(proj, row2(mla_cq_g[l]), row2(mla_ckv_g[l]), wuq[l], wuk[l], wuv[l],
                                             tabs, tm, tok_rows, seq)
        gw = ATT_GROUPS * LANES
        n_s = 2 * ATT_GROUPS
        o_sb = _attn_call(_sb_kernel, "sb_attn", proj, gw, C_SBQ // gw, proj, gw, C_SBK // gw,
                          proj, C_SBV // gw, [], _sb_scratch(n_s), batch, seq, WIDTH // gw)
        o_mla = _attn_call(_mla_kernel, "mla_attn", q_m, 2 * gw, 0, k_m, 2 * gw, 0, v_m, 0, [],
                           _softmax_scratch(n_s), batch, seq, WIDTH // gw)
        lam_init = 0.8 - 0.6 * math.exp(-0.3 * l)
        o_diff = _attn_call(functools.partial(_diff_kernel, lam_init=lam_init), "diff_attn",
                            q_d, gw, 0, k_d, gw, 0, proj, C_DV // gw,
                            [diff_lambda[l].astype(F32), row2(diff_norm_g[l])], _softmax_scratch(n_s), batch, seq,
                            WIDTH // gw)
        g_next = row2(final_g if last else norm_g[l + 1])
        res = _merge_call(o_sb, o_mla, o_diff, proj, h, row2(b_gate[l]), w_o_sb[l].astype(BF16),
                          w_o_mla[l].astype(BF16), w_o_diff[l].astype(BF16), w_out[l].astype(BF16), g_next,
                          tm, tok_rows if last else rows, last)
        if last:
            out = res[0]
        else:
            h, hn = res
    return out.reshape(batch, seq, d)
```

```python
import functools
import math

import jax
import jax.numpy as jnp
from jax import lax
from jax.experimental import pallas as pl
from jax.experimental.pallas import tpu as pltpu

F32 = jnp.float32
BF16 = jnp.bfloat16

D_MODEL = 1024
CHUNK = 64
N_META = 16
ROPE_THETA = 500000.0
EPS = 1e-6
LANES = 128
HEAD = 64
WIDTH = 512
MLA_HEADS = 8
MLA_ROPE = 32
MLA_Q_RANK = 384
MLA_KV_RANK = 256
DIFF_ROT = 16

META_ROWS = 256
ROW_TILE = 256
ATT_TILE = 256
ROW_CHUNK = 128
ATT_GROUPS = 4
NEG_BIG = -1e30
SB_DEAD_RUN = -104.0
VMEM_LIMIT = 56 * 1024 * 1024

C_GATE = 0
C_SBQ = 3072
C_SBK = 3584
C_SBV = 4096
C_SBZ = 4608
C_DQ = 5120
C_DK = 5632
C_DV = 6144
C_DZ = 6656
C_MZ = 7168
C_LAT = 7680
LAT_W = 768
N_PROJ = 8448
PROJ_TN = 768


def _params(sem):
    return pltpu.CompilerParams(dimension_semantics=sem, vmem_limit_bytes=VMEM_LIMIT)


def _rms(x32, g):
    ms = jnp.mean(x32 * x32, axis=-1, keepdims=True)
    return x32 * lax.rsqrt(ms + EPS) * g


def _norm_kernel(h_ref, g_ref, o_ref):
    o_ref[...] = _rms(h_ref[...], g_ref[...]).astype(o_ref.dtype)


def _norm_call(h, g, tm):
    rows = h.shape[0]
    return pl.pallas_call(
        _norm_kernel,
        grid=(rows // tm,),
        in_specs=[pl.BlockSpec((tm, D_MODEL), lambda i: (i, 0)),
                  pl.BlockSpec((1, D_MODEL), lambda i: (0, 0))],
        out_specs=pl.BlockSpec((tm, D_MODEL), lambda i: (i, 0)),
        out_shape=jax.ShapeDtypeStruct((rows, D_MODEL), BF16),
        compiler_params=_params(("parallel",)),
        name="norm_in",
    )(h, g)


def _proj_kernel(a_ref, w_ref, o_ref):
    o_ref[...] = jnp.dot(a_ref[...], w_ref[...], preferred_element_type=F32).astype(o_ref.dtype)


def _proj_call(hn, w, tm):
    rows = hn.shape[0]
    return pl.pallas_call(
        _proj_kernel,
        grid=(rows // tm, N_PROJ // PROJ_TN),
        in_specs=[pl.BlockSpec((tm, D_MODEL), lambda i, j: (i, 0)),
                  pl.BlockSpec((D_MODEL, PROJ_TN), lambda i, j: (0, j))],
        out_specs=pl.BlockSpec((tm, PROJ_TN), lambda i, j: (i, j)),
        out_shape=jax.ShapeDtypeStruct((rows, N_PROJ), BF16),
        compiler_params=_params(("parallel", "arbitrary")),
        name="in_proj",
    )(hn, w)


def _rot(x, c, s1, s2, shift):
    return x * c + pltpu.roll(x, LANES - shift, 1) * s1 + pltpu.roll(x, shift, 1) * s2


def _prep_kernel(lat_ref, dq_ref, dk_ref, gq_ref, gkv_ref, wuq_ref, wuk_ref, wuv_ref,
                 cm_ref, s1m_ref, s2m_ref, cd_ref, s1d_ref, s2d_ref,
                 qm_ref, km_ref, vm_ref, dqo_ref, dko_ref):
    lat = lat_ref[...].astype(F32)
    ckv = lat[:, :MLA_KV_RANK]
    cq = lat[:, MLA_KV_RANK:MLA_KV_RANK + MLA_Q_RANK]
    kr = lat[:, MLA_KV_RANK + MLA_Q_RANK:]
    ncq = _rms(cq, gq_ref[...]).astype(BF16)
    nckv = _rms(ckv, gkv_ref[...]).astype(BF16)
    cm, s1m, s2m = cm_ref[...], s1m_ref[...], s2m_ref[...]
    scale_b = 1.0 / math.sqrt(HEAD + MLA_ROPE)
    k_rope = _rot(kr, cm, s1m, s2m, MLA_ROPE // 2)
    vm_ref[...] = jnp.dot(nckv, wuv_ref[...], preferred_element_type=F32).astype(BF16)
    for h in range(MLA_HEADS):
        sl = slice(h * LANES, (h + 1) * LANES)
        qf = jnp.dot(ncq, wuq_ref[:, sl], preferred_element_type=F32)
        qm_ref[:, sl] = (_rot(qf, cm, s1m, s2m, MLA_ROPE // 2) * scale_b).astype(BF16)
        kf = jnp.dot(nckv, wuk_ref[:, sl], preferred_element_type=F32)
        km_ref[:, sl] = (kf + k_rope).astype(BF16)
    cd, s1d, s2d = cd_ref[...], s1d_ref[...], s2d_ref[...]
    for h in range(WIDTH // LANES):
        sl = slice(h * LANES, (h + 1) * LANES)
        dqo_ref[:, sl] = _rot(dq_ref[:, sl].astype(F32), cd, s1d, s2d, DIFF_ROT // 2).astype(BF16)
        dko_ref[:, sl] = _rot(dk_ref[:, sl].astype(F32), cd, s1d, s2d, DIFF_ROT // 2).astype(BF16)


def _prep_call(proj, gq, gkv, wuq, wuk, wuv, tabs, tm, tok_rows, seq):
    rows = proj.shape[0]
    n_tok_tiles = tok_rows // tm
    per_seq = seq // tm

    def tab_map(i):
        return (jnp.where(i < n_tok_tiles, i % per_seq, per_seq), 0)

    row = lambda w, c: pl.BlockSpec((tm, w), lambda i: (i, c))
    full = lambda a: pl.BlockSpec(a.shape, lambda i: (0, 0))
    tab = pl.BlockSpec((tm, LANES), tab_map)
    out = lambda w: jax.ShapeDtypeStruct((rows, w), BF16)
    return pl.pallas_call(
        _prep_kernel,
        grid=(rows // tm,),
        in_specs=[row(LAT_W, C_LAT // LAT_W), row(WIDTH, C_DQ // WIDTH), row(WIDTH, C_DK // WIDTH),
                  full(gq), full(gkv), full(wuq), full(wuk), full(wuv)] + [tab] * 6,
        out_specs=[row(MLA_HEADS * LANES, 0), row(MLA_HEADS * LANES, 0), row(WIDTH, 0),
                   row(WIDTH, 0), row(WIDTH, 0)],
        out_shape=[out(MLA_HEADS * LANES), out(MLA_HEADS * LANES), out(WIDTH), out(WIDTH), out(WIDTH)],
        compiler_params=_params(("parallel",)),
        name="prep",
    )(proj, proj, proj, gq, gkv, wuq, wuk, wuv, *tabs)


def _sigmoid(x):
    return 1.0 / (1.0 + jnp.exp(-x))


def _merge_kernel(osb_ref, omla_ref, odiff_ref, zsb_ref, zmla_ref, zdiff_ref, g0_ref, g1_ref, g2_ref,
                  h_ref, bg_ref, wsb_ref, wmla_ref, wdiff_ref, wout_ref, gn_ref, *out_refs, last):
    def branch(o_ref, z_ref, w_ref):
        z = z_ref[...].astype(F32)
        a = o_ref[...].astype(F32) * (z * _sigmoid(z))
        return jnp.dot(a.astype(BF16), w_ref[...], preferred_element_type=F32)

    bg = bg_ref[...]
    merged = _sigmoid(g0_ref[...].astype(F32) + bg[:, :D_MODEL]) * branch(osb_ref, zsb_ref, wsb_ref)
    merged += _sigmoid(g1_ref[...].astype(F32) + bg[:, D_MODEL:2 * D_MODEL]) * branch(omla_ref, zmla_ref, wmla_ref)
    merged += _sigmoid(g2_ref[...].astype(F32) + bg[:, 2 * D_MODEL:]) * branch(odiff_ref, zdiff_ref, wdiff_ref)
    h_new = h_ref[...] + jnp.dot(merged.astype(BF16), wout_ref[...], preferred_element_type=F32)
    normed = _rms(h_new, gn_ref[...])
    if last:
        out_refs[0][...] = normed
    else:
        out_refs[0][...] = h_new
        out_refs[1][...] = normed.astype(BF16)


def _merge_call(o_sb, o_mla, o_diff, proj, h, bg, w_sb, w_mla, w_diff, w_out, g_next, tm, out_rows, last):
    row = lambda w, c: pl.BlockSpec((tm, w), lambda i: (i, c))
    full = lambda a: pl.BlockSpec(a.shape, lambda i: (0, 0))
    if last:
        out_specs = [row(D_MODEL, 0)]
        out_shape = [jax.ShapeDtypeStruct((out_rows, D_MODEL), F32)]
    else:
        out_specs = [row(D_MODEL, 0), row(D_MODEL, 0)]
        out_shape = [jax.ShapeDtypeStruct((out_rows, D_MODEL), F32),
                     jax.ShapeDtypeStruct((out_rows, D_MODEL), BF16)]
    return pl.pallas_call(
        functools.partial(_merge_kernel, last=last),
        grid=(out_rows // tm,),
        in_specs=[row(WIDTH, 0), row(WIDTH, 0), row(WIDTH, 0),
                  row(WIDTH, C_SBZ // WIDTH), row(WIDTH, C_MZ // WIDTH), row(WIDTH, C_DZ // WIDTH),
                  row(D_MODEL, 0), row(D_MODEL, 1), row(D_MODEL, 2),
                  row(D_MODEL, 0), full(bg), full(w_sb), full(w_mla), full(w_diff), full(w_out), full(g_next)],
        out_specs=out_specs,
        out_shape=out_shape,
        compiler_params=_params(("parallel",)),
        name="merge_last" if last else "merge",
    )(o_sb, o_mla, o_diff, proj, proj, proj, proj, proj, proj, h, bg, w_sb, w_mla, w_diff, w_out, g_next)


def _qk(q, k):
    return lax.dot_general(q, k, (((1,), (1,)), ((), ())), preferred_element_type=F32)


def _iota2(shape, axis):
    return lax.broadcasted_iota(jnp.int32, shape, axis)


def _half_mask(rows, half):
    lane = _iota2((rows, LANES), 1)
    return (lane >= HEAD) if half else (lane < HEAD)


def _pv(p, v):
    return jnp.dot(p.astype(BF16), v, preferred_element_type=F32)


def _incl_tri(n):
    return (_iota2((n, n), 0) >= _iota2((n, n), 1)).astype(BF16)


def _sb_logw(z, tri):
    lk = jnp.minimum(-z, 0.0) - jnp.log(1.0 + jnp.exp(-jnp.abs(z)))
    hi = lk.astype(BF16)
    lo = (lk - hi.astype(F32)).astype(BF16)
    cum = jnp.dot(hi, tri, preferred_element_type=F32) + jnp.dot(lo, tri, preferred_element_type=F32)
    return z + cum, cum[:, 0:1]


def _wide(x, width):
    return jnp.concatenate([x] * (width // LANES), axis=-1)


def _row_blocks(t):
    return [pl.ds(r * ROW_CHUNK, ROW_CHUNK) for r in range(t // ROW_CHUNK)]


def _stream_cols(n_streams):
    return [slice((n // 2) * LANES, (n // 2 + 1) * LANES) for n in range(n_streams)]


def _half_queries(q_ref, t):
    qs = []
    for n, c in enumerate(_stream_cols(2 * (q_ref.shape[1] // LANES))):
        qg = q_ref[:, c] * 0.125
        qs.append(jnp.where(_half_mask(t, n % 2), qg, jnp.zeros_like(qg)))
    return qs


def _join_halves(o_ref, outs, t):
    for g in range(len(outs) // 2):
        o_ref[:, g * LANES:(g + 1) * LANES] = jnp.where(_half_mask(t, 0), outs[2 * g], outs[2 * g + 1]).astype(o_ref.dtype)


def _sb_kernel(q_ref, k_ref, v_ref, km_ref, vm_ref, o_ref, z_scr, w_scr, acc_scr, run_scr, mp_scr, *, n_tok_tiles):
    i = pl.program_id(2)
    t = ATT_TILE
    qs = _half_queries(q_ref, t)
    n_s = len(qs)
    cols = _stream_cols(n_s)
    tri = _incl_tri(t)
    tri_m = _incl_tri(N_META)
    kms = [km_ref[:, c] for c in cols]
    vms = [vm_ref[:, c] for c in cols]
    blocks = _row_blocks(t)

    def tiles_at(ref, p):
        rows = pl.ds(pl.multiple_of(jnp.maximum(i - p, 0) * t, t), t)
        per_group = [ref[rows, cols[2 * g]] for g in range(n_s // 2)]
        return [per_group[n // 2] for n in range(n_s)]

    def scores(p):
        return [_qk(qh, kb) for qh, kb in zip(qs, tiles_at(k_ref, p))]

    def fold(vbs):
        loaded = [(n, r, w_scr[n, r, :], acc_scr[n, r, :]) for n in range(n_s) for r in blocks]
        return [(acc_scr, n, r, acc + _pv(jnp.exp(logw), vbs[n])) for n, r, logw, acc in loaded]

    def advance(zs, runs, masks):
        out = []
        for n in range(n_s):
            for b, r in enumerate(blocks):
                z, run = zs[n][b], runs[n][b]
                if masks is not None:
                    z = jnp.where(masks[b], z, NEG_BIG)
                part, total = _sb_logw(z, tri)
                out += [(w_scr, n, r, part + _wide(run, t)), (run_scr, n, r, run + total)]
        return out

    def store(writes):
        for ref, n, r, val in writes:
            ref[n, r, :] = val

    def tok():
        z0, z1 = scores(0), scores(1)
        zm = [_qk(qh, km) for qh, km in zip(qs, kms)]
        causal = [_iota2((ROW_CHUNK, t), 1) < _iota2((ROW_CHUNK, t), 0) + b * ROW_CHUNK for b in range(len(blocks))]
        acc_scr[...] = jnp.zeros_like(acc_scr)
        zero_run = jnp.zeros((ROW_CHUNK, LANES), F32)
        z0_blocks = [[z[b * ROW_CHUNK:(b + 1) * ROW_CHUNK] for b in range(len(blocks))] for z in z0]
        first = advance(z0_blocks, [[zero_run] * len(blocks)] * n_s, causal)
        store(first)
        for n in range(n_s):
            z_scr[n] = z1[n]
            mp_scr[n] = _sb_logw(zm[n], tri_m)[0]

        def largest_run(writes):
            runs = [val for ref, _, _, val in writes if ref is run_scr]
            return jnp.max(functools.reduce(jnp.maximum, runs))

        def alive(c):
            return jnp.logical_and(c[0] < i, c[1] >= SB_DEAD_RUN)

        def body(c):
            p = c[0]
            z_next = scores(p + 2)
            zs = [[z_scr[n, r, :] for r in blocks] for n in range(n_s)]
            runs = [[run_scr[n, r, :] for r in blocks] for n in range(n_s)]
            adv = advance(zs, runs, None)
            store(fold(tiles_at(v_ref, p)) + adv)
            for n in range(n_s):
                z_scr[n] = z_next[n]
            return p + 1, largest_run(adv)

        p_end, _ = lax.while_loop(alive, body, (jnp.int32(0), largest_run(first)))
        store(fold(tiles_at(v_ref, p_end)))
        return [acc_scr[n] + _pv(jnp.exp(mp_scr[n] + run_scr[n][:, :N_META]), vms[n]) for n in range(n_s)]

    def meta():
        causal_m = _iota2((t, N_META), 1) < _iota2((t, N_META), 0)
        zm = [_qk(qh, km) for qh, km in zip(qs, kms)]
        return [_pv(jnp.exp(_sb_logw(jnp.where(causal_m, z, NEG_BIG), tri_m)[0]), vm) for z, vm in zip(zm, vms)]

    _join_halves(o_ref, lax.cond(i < n_tok_tiles, tok, meta), t)


def _sb_scratch(n_s):
    t = ATT_TILE
    return [pltpu.VMEM((n_s, t, t), F32), pltpu.VMEM((n_s, t, t), F32), pltpu.VMEM((n_s, t, LANES), F32),
            pltpu.VMEM((n_s, t, LANES), F32), pltpu.VMEM((n_s, t, N_META), F32)]


def _row_max(s):
    return jnp.max(s, axis=-1, keepdims=True)


def _row_sum(p):
    return jnp.sum(p, axis=-1, keepdims=True)


def _softmax_attend(i, n_tok_tiles, qs, k_ats, v_ats, kms, vms, s_scr, acc_scr, m_scr, l_scr):
    t = ATT_TILE
    n_s = len(qs)
    blocks = _row_blocks(t)

    def fold(vbs):
        for n in range(n_s):
            for r in blocks:
                s, m_old, l_old, acc_old = s_scr[n, r, :], m_scr[n, r, :], l_scr[n, r, :], acc_scr[n, r, :]
                m_new = jnp.maximum(m_old, _row_max(s))
                alpha = jnp.exp(m_old - m_new)
                p = jnp.exp(s - _wide(m_new, t))
                m_scr[n, r, :] = m_new
                l_scr[n, r, :] = alpha * l_old + _row_sum(p)
                acc_scr[n, r, :] = alpha * acc_old + _pv(p, vbs[n])

    def scores(j):
        st = pl.multiple_of(j * t, t)
        return [_qk(q, k_at(st)) for q, k_at in zip(qs, k_ats)]

    def values(j):
        st = pl.multiple_of(j * t, t)
        return [v_at(st) for v_at in v_ats]

    def tok():
        sds, s0s = scores(i), scores(0)
        sms = [_qk(q, km) for q, km in zip(qs, kms)]
        vds = values(i)
        for n in range(n_s):
            for b, r in enumerate(blocks):
                rows = slice(b * ROW_CHUNK, (b + 1) * ROW_CHUNK)
                q_chunk = _iota2((ROW_CHUNK, t), 0) // CHUNK + b * (ROW_CHUNK // CHUNK)
                sd = jnp.where(_iota2((ROW_CHUNK, t), 1) // CHUNK <= q_chunk, sds[n][rows], NEG_BIG)
                sm = sms[n][rows]
                m = jnp.maximum(_row_max(sd), _row_max(sm))
                pd, pm = jnp.exp(sd - m), jnp.exp(sm - m)
                m_scr[n, r, :] = jnp.broadcast_to(m, (ROW_CHUNK, LANES))
                l_scr[n, r, :] = jnp.broadcast_to(_row_sum(pd) + _row_sum(pm), (ROW_CHUNK, LANES))
                acc_scr[n, r, :] = _pv(pd, vds[n]) + _pv(pm, vms[n])
        for n in range(n_s):
            s_scr[n] = s0s[n]

        def body(j, carry):
            s_next = scores(jnp.minimum(j + 1, i - 1))
            fold(values(j))
            for n in range(n_s):
                s_scr[n] = s_next[n]
            return carry

        lax.fori_loop(0, i, body, 0)
        return [acc_scr[n] * (1.0 / l_scr[n]) for n in range(n_s)]

    def meta():
        sms = [_qk(q, km) for q, km in zip(qs, kms)]
        pms = [jnp.exp(sm - _row_max(sm)) for sm in sms]
        return [_pv(pm, vm) * (1.0 / _row_sum(pm)) for pm, vm in zip(pms, vms)]

    return lax.cond(i < n_tok_tiles, tok, meta)


def _softmax_scratch(n_s):
    t = ATT_TILE
    return [pltpu.VMEM((n_s, t, t), F32), pltpu.VMEM((n_s, t, LANES), F32), pltpu.VMEM((n_s, t, LANES), F32),
            pltpu.VMEM((n_s, t, LANES), F32)]


def _mla_kernel(q_ref, k_ref, v_ref, km_ref, vm_ref, o_ref, *scratch, n_tok_tiles):
    i = pl.program_id(2)
    t = ATT_TILE
    n_s = q_ref.shape[1] // LANES
    qcols = [slice(n * LANES, (n + 1) * LANES) for n in range(n_s)]
    vcols = _stream_cols(n_s)
    outs = _softmax_attend(
        i, n_tok_tiles, [q_ref[:, c] for c in qcols],
        [lambda st, c=c: k_ref[pl.ds(st, t), c] for c in qcols], [lambda st, c=c: v_ref[pl.ds(st, t), c] for c in vcols],
        [km_ref[:, c] for c in qcols], [vm_ref[:, c] for c in vcols], *scratch)
    _join_halves(o_ref, outs, t)


def _diff_kernel(q_ref, k_ref, v_ref, km_ref, vm_ref, lam_ref, g_ref, o_ref, *scratch, n_tok_tiles, lam_init):
    i = pl.program_id(2)
    t = ATT_TILE
    qs = _half_queries(q_ref, t)
    cols = _stream_cols(len(qs))
    outs = _softmax_attend(
        i, n_tok_tiles, qs,
        [lambda st, c=c: k_ref[pl.ds(st, t), c] for c in cols], [lambda st, c=c: v_ref[pl.ds(st, t), c] for c in cols],
        [km_ref[:, c] for c in cols], [vm_ref[:, c] for c in cols], *scratch)
    lam4 = lam_ref[...]
    lam = (jnp.exp(jnp.sum(lam4[0:1] * lam4[1:2], axis=-1, keepdims=True))
           - jnp.exp(jnp.sum(lam4[2:3] * lam4[3:4], axis=-1, keepdims=True)) + lam_init)
    for g in range(len(qs) // 2):
        o = outs[2 * g] - lam * outs[2 * g + 1]
        o_ref[:, g * LANES:(g + 1) * LANES] = (_rms(o, g_ref[...]) * (1.0 - lam_init)).astype(o_ref.dtype)


def _attn_call(kernel, name, q, qw, qc, k, kw, kc, v, vc, extra, scratch, batch, seq, groups):
    rows = q.shape[0]
    t = ATT_TILE
    n_tok_tiles = seq // t
    tok_rows = batch * seq
    meta_q = tok_rows // t
    meta_k = tok_rows // N_META
    vw = WIDTH // groups

    def q_map(b, g, i):
        return (jnp.where(i < n_tok_tiles, b * n_tok_tiles + i, meta_q), qc + g)

    def o_map(b, g, i):
        return (jnp.where(i < n_tok_tiles, b * n_tok_tiles + i, meta_q + b), g)

    in_specs = [
        pl.BlockSpec((t, qw), q_map),
        pl.BlockSpec((seq, kw), lambda b, g, i: (b, kc + g)),
        pl.BlockSpec((seq, vw), lambda b, g, i: (b, vc + g)),
        pl.BlockSpec((N_META, kw), lambda b, g, i: (meta_k, kc + g)),
        pl.BlockSpec((N_META, vw), lambda b, g, i: (meta_k, vc + g)),
    ] + [pl.BlockSpec(a.shape, lambda b, g, i: (0, 0)) for a in extra]
    return pl.pallas_call(
        functools.partial(kernel, n_tok_tiles=n_tok_tiles),
        grid=(batch, groups, n_tok_tiles + 1),
        in_specs=in_specs,
        out_specs=pl.BlockSpec((t, vw), o_map),
        out_shape=jax.ShapeDtypeStruct((rows + (batch - 1) * t, WIDTH), BF16),
        scratch_shapes=scratch,
        compiler_params=_params(("parallel", "parallel", "arbitrary")),
        name=name,
    )(q, k, v, k, v, *extra)


def _perm_w_in(w_in):
    sizes = [512, 512, 512, 512, 384, 256, 32, 512, 512, 512, 512, 512, 3072]
    offs = [0]
    for s in sizes:
        offs.append(offs[-1] + s)
    seg = [w_in[..., offs[n]:offs[n + 1]] for n in range(len(sizes))]
    sb_q, sb_k, sb_v, sb_z, cq, ckv, kr, mz, dq, dk, dv, dz, gate = seg
    pad = lambda n: jnp.zeros(w_in.shape[:-1] + (n,), w_in.dtype)
    cols = [gate, sb_q, sb_k, sb_v, sb_z, dq, dk, dv, dz, mz, ckv, cq, pad(HEAD), kr, pad(LANES - HEAD - MLA_ROPE)]
    return jnp.concatenate(cols, axis=-1).astype(BF16)


def _rope_tables(seq):
    pos = jnp.concatenate([jnp.arange(seq) + N_META, jnp.arange(META_ROWS)]).astype(F32)[:, None]

    def tables(dim, starts):
        inv = ROPE_THETA ** (-jnp.arange(0, dim, 2, dtype=F32) / dim)
        ang = pos * inv[None, :]
        cos, sin = jnp.cos(ang), jnp.sin(ang)
        half = dim // 2
        c = jnp.ones((pos.shape[0], LANES), F32)
        s1 = jnp.zeros((pos.shape[0], LANES), F32)
        s2 = jnp.zeros((pos.shape[0], LANES), F32)
        for st in starts:
            c = c.at[:, st:st + half].set(cos).at[:, st + half:st + dim].set(cos)
            s1 = s1.at[:, st:st + half].set(-sin)
            s2 = s2.at[:, st + half:st + dim].set(sin)
        return [c, s1, s2]

    return tables(MLA_ROPE, [HEAD]) + tables(DIFF_ROT, [0, HEAD])


def _pick(rows, candidates):
    for c in candidates:
        if rows % c == 0:
            return c
    raise ValueError(f"no row tile for {rows}")


def kernel(x, meta_tokens, norm_g, w_in, b_gate, mla_cq_g, mla_ckv_g, mla_w_uq, mla_w_ukv, diff_lambda,
           diff_norm_g, w_o_sb, w_o_mla, w_o_diff, w_out, final_g):
    batch, seq, d = x.shape
    depth = norm_g.shape[0]
    assert d == D_MODEL and seq % ATT_TILE == 0 and meta_tokens.shape == (N_META, D_MODEL)
    tok_rows = batch * seq
    rows = tok_rows + META_ROWS
    tm = ROW_TILE
    tm_proj = _pick(rows, (1280, 1024, 768, 512, 256))

    h = jnp.concatenate([x.reshape(tok_rows, d), meta_tokens.astype(x.dtype),
                         jnp.zeros((META_ROWS - N_META, d), x.dtype)], axis=0)
    w_in_p = _perm_w_in(w_in)
    wuq = jnp.pad(mla_w_uq.reshape(depth, MLA_Q_RANK, MLA_HEADS, HEAD + MLA_ROPE),
                  ((0, 0), (0, 0), (0, 0), (0, LANES - HEAD - MLA_ROPE))).reshape(depth, MLA_Q_RANK, -1).astype(BF16)
    wukv = mla_w_ukv.reshape(depth, MLA_KV_RANK, MLA_HEADS, 2 * HEAD)
    wuk = jnp.pad(wukv[..., :HEAD], ((0, 0), (0, 0), (0, 0), (0, LANES - HEAD))).reshape(depth, MLA_KV_RANK, -1).astype(BF16)
    wuv = wukv[..., HEAD:].reshape(depth, MLA_KV_RANK, -1).astype(BF16)
    tabs = _rope_tables(seq)
    row2 = lambda a: a.reshape(1, -1)

    hn = _norm_call(h, row2(norm_g[0]), tm)
    out = None
    for l in range(depth):
        last = l == depth - 1
        proj = _proj_call(hn, w_in_p[l], tm_proj)
        q_m, k_m, v_m, q_d, k_d = _prep_call(proj, row2(mla_cq_g[l]), row2(mla_ckv_g[l]), wuq[l], wuk[l], wuv[l],
                                             tabs, tm, tok_rows, seq)
        gw = ATT_GROUPS * LANES
        n_s = 2 * ATT_GROUPS
        o_sb = _attn_call(_sb_kernel, "sb_attn", proj, gw, C_SBQ // gw, proj, gw, C_SBK // gw,
                          proj, C_SBV // gw, [], _sb_scratch(n_s), batch, seq, WIDTH // gw)
        o_mla = _attn_call(_mla_kernel, "mla_attn", q_m, 2 * gw, 0, k_m, 2 * gw, 0, v_m, 0, [],
                           _softmax_scratch(n_s), batch, seq, WIDTH // gw)
        lam_init = 0.8 - 0.6 * math.exp(-0.3 * l)
        o_diff = _attn_call(functools.partial(_diff_kernel, lam_init=lam_init), "diff_attn",
                            q_d, gw, 0, k_d, gw, 0, proj, C_DV // gw,
                            [diff_lambda[l].astype(F32), row2(diff_norm_g[l])], _softmax_scratch(n_s), batch, seq,
                            WIDTH // gw)
        g_next = row2(final_g if last else norm_g[l + 1])
        res = _merge_call(o_sb, o_mla, o_diff, proj, h, row2(b_gate[l]), w_o_sb[l].astype(BF16),
                          w_o_mla[l].astype(BF16), w_o_diff[l].astype(BF16), w_out[l].astype(BF16), g_next,
                          tm, tok_rows if last else rows, last)
        if last:
            out = res[0]
        else:
            h, hn = res
    return out.reshape(batch, seq, d)
```

```python
import functools
import math

import jax
import jax.numpy as jnp
from jax import lax
from jax.experimental import pallas as pl
from jax.experimental.pallas import tpu as pltpu

F32 = jnp.float32
BF16 = jnp.bfloat16

D_MODEL = 1024
CHUNK = 64
N_META = 16
ROPE_THETA = 500000.0
EPS = 1e-6
LANES = 128
HEAD = 64
WIDTH = 512
MLA_HEADS = 8
MLA_ROPE = 32
MLA_Q_RANK = 384
MLA_KV_RANK = 256
DIFF_ROT = 16

META_ROWS = 256
ROW_TILE = 256
ATT_TILE = 256
ROW_CHUNK = 128
ATT_GROUPS = 4
NEG_BIG = -1e30
LOG2E = 1.4426950408889634
SB_DEAD_RUN = -104.0
VMEM_LIMIT = 56 * 1024 * 1024

C_GATE = 0
C_SBQ = 3072
C_SBK = 3584
C_SBV = 4096
C_SBZ = 4608
C_DQ = 5120
C_DK = 5632
C_DV = 6144
C_DZ = 6656
C_MZ = 7168
C_LAT = 7680
LAT_W = 768
N_PROJ = 8448
PROJ_TN = 768


def _params(sem):
    return pltpu.CompilerParams(dimension_semantics=sem, vmem_limit_bytes=VMEM_LIMIT)


def _rms(x32, g):
    ms = jnp.mean(x32 * x32, axis=-1, keepdims=True)
    return x32 * lax.rsqrt(ms + EPS) * g


def _norm_kernel(h_ref, g_ref, o_ref):
    o_ref[...] = _rms(h_ref[...], g_ref[...]).astype(o_ref.dtype)


def _norm_call(h, g, tm):
    rows = h.shape[0]
    return pl.pallas_call(
        _norm_kernel,
        grid=(rows // tm,),
        in_specs=[pl.BlockSpec((tm, D_MODEL), lambda i: (i, 0)),
                  pl.BlockSpec((1, D_MODEL), lambda i: (0, 0))],
        out_specs=pl.BlockSpec((tm, D_MODEL), lambda i: (i, 0)),
        out_shape=jax.ShapeDtypeStruct((rows, D_MODEL), BF16),
        compiler_params=_params(("parallel",)),
        name="norm_in",
    )(h, g)


def _proj_kernel(a_ref, w_ref, o_ref):
    o_ref[...] = jnp.dot(a_ref[...], w_ref[...], preferred_element_type=F32).astype(o_ref.dtype)


def _proj_call(hn, w, tm):
    rows = hn.shape[0]
    return pl.pallas_call(
        _proj_kernel,
        grid=(rows // tm, N_PROJ // PROJ_TN),
        in_specs=[pl.BlockSpec((tm, D_MODEL), lambda i, j: (i, 0)),
                  pl.BlockSpec((D_MODEL, PROJ_TN), lambda i, j: (0, j))],
        out_specs=pl.BlockSpec((tm, PROJ_TN), lambda i, j: (i, j)),
        out_shape=jax.ShapeDtypeStruct((rows, N_PROJ), BF16),
        compiler_params=_params(("parallel", "arbitrary")),
        name="in_proj",
    )(hn, w)


def _rot(x, c, s1, s2, shift):
    return x * c + pltpu.roll(x, LANES - shift, 1) * s1 + pltpu.roll(x, shift, 1) * s2


def _prep_kernel(lat_ref, dq_ref, dk_ref, gq_ref, gkv_ref, wuq_ref, wuk_ref, wuv_ref,
                 cm_ref, s1m_ref, s2m_ref, cd_ref, s1d_ref, s2d_ref,
                 qm_ref, km_ref, vm_ref, dqo_ref, dko_ref):
    lat = lat_ref[...].astype(F32)
    ckv = lat[:, :MLA_KV_RANK]
    cq = lat[:, MLA_KV_RANK:MLA_KV_RANK + MLA_Q_RANK]
    kr = lat[:, MLA_KV_RANK + MLA_Q_RANK:]
    ncq = _rms(cq, gq_ref[...]).astype(BF16)
    nckv = _rms(ckv, gkv_ref[...]).astype(BF16)
    cm, s1m, s2m = cm_ref[...], s1m_ref[...], s2m_ref[...]
    scale_b = LOG2E / math.sqrt(HEAD + MLA_ROPE)
    k_rope = _rot(kr, cm, s1m, s2m, MLA_ROPE // 2)
    vm_ref[...] = jnp.dot(nckv, wuv_ref[...], preferred_element_type=F32).astype(BF16)
    for h in range(MLA_HEADS):
        sl = slice(h * LANES, (h + 1) * LANES)
        qf = jnp.dot(ncq, wuq_ref[:, sl], preferred_element_type=F32)
        qm_ref[:, sl] = (_rot(qf, cm, s1m, s2m, MLA_ROPE // 2) * scale_b).astype(BF16)
        kf = jnp.dot(nckv, wuk_ref[:, sl], preferred_element_type=F32)
        km_ref[:, sl] = (kf + k_rope).astype(BF16)
    cd, s1d, s2d = cd_ref[...], s1d_ref[...], s2d_ref[...]
    for h in range(WIDTH // LANES):
        sl = slice(h * LANES, (h + 1) * LANES)
        dqo_ref[:, sl] = (_rot(dq_ref[:, sl].astype(F32), cd, s1d, s2d, DIFF_ROT // 2) * (LOG2E / math.sqrt(HEAD))).astype(BF16)
        dko_ref[:, sl] = _rot(dk_ref[:, sl].astype(F32), cd, s1d, s2d, DIFF_ROT // 2).astype(BF16)


def _prep_call(proj, gq, gkv, wuq, wuk, wuv, tabs, tm, tok_rows, seq):
    rows = proj.shape[0]
    n_tok_tiles = tok_rows // tm
    per_seq = seq // tm

    def tab_map(i):
        return (jnp.where(i < n_tok_tiles, i % per_seq, per_seq), 0)

    row = lambda w, c: pl.BlockSpec((tm, w), lambda i: (i, c))
    full = lambda a: pl.BlockSpec(a.shape, lambda i: (0, 0))
    tab = pl.BlockSpec((tm, LANES), tab_map)
    out = lambda w: jax.ShapeDtypeStruct((rows, w), BF16)
    return pl.pallas_call(
        _prep_kernel,
        grid=(rows // tm,),
        in_specs=[row(LAT_W, C_LAT // LAT_W), row(WIDTH, C_DQ // WIDTH), row(WIDTH, C_DK // WIDTH),
                  full(gq), full(gkv), full(wuq), full(wuk), full(wuv)] + [tab] * 6,
        out_specs=[row(MLA_HEADS * LANES, 0), row(MLA_HEADS * LANES, 0), row(WIDTH, 0),
                   row(WIDTH, 0), row(WIDTH, 0)],
        out_shape=[out(MLA_HEADS * LANES), out(MLA_HEADS * LANES), out(WIDTH), out(WIDTH), out(WIDTH)],
        compiler_params=_params(("parallel",)),
        name="prep",
    )(proj, proj, proj, gq, gkv, wuq, wuk, wuv, *tabs)


def _sigmoid(x):
    return 1.0 / (1.0 + jnp.exp(-x))


def _merge_kernel(osb_ref, omla_ref, odiff_ref, zsb_ref, zmla_ref, zdiff_ref, g0_ref, g1_ref, g2_ref,
                  h_ref, bg_ref, wsb_ref, wmla_ref, wdiff_ref, wout_ref, gn_ref, *out_refs, last):
    def branch(o_ref, z_ref, w_ref):
        z = z_ref[...].astype(F32)
        a = o_ref[...].astype(F32) * (z * _sigmoid(z))
        return jnp.dot(a.astype(BF16), w_ref[...], preferred_element_type=F32)

    bg = bg_ref[...]
    merged = _sigmoid(g0_ref[...].astype(F32) + bg[:, :D_MODEL]) * branch(osb_ref, zsb_ref, wsb_ref)
    merged += _sigmoid(g1_ref[...].astype(F32) + bg[:, D_MODEL:2 * D_MODEL]) * branch(omla_ref, zmla_ref, wmla_ref)
    merged += _sigmoid(g2_ref[...].astype(F32) + bg[:, 2 * D_MODEL:]) * branch(odiff_ref, zdiff_ref, wdiff_ref)
    h_new = h_ref[...] + jnp.dot(merged.astype(BF16), wout_ref[...], preferred_element_type=F32)
    normed = _rms(h_new, gn_ref[...])
    if last:
        out_refs[0][...] = normed
    else:
        out_refs[0][...] = h_new
        out_refs[1][...] = normed.astype(BF16)


def _merge_call(o_sb, o_mla, o_diff, proj, h, bg, w_sb, w_mla, w_diff, w_out, g_next, tm, out_rows, last):
    row = lambda w, c: pl.BlockSpec((tm, w), lambda i: (i, c))
    full = lambda a: pl.BlockSpec(a.shape, lambda i: (0, 0))
    if last:
        out_specs = [row(D_MODEL, 0)]
        out_shape = [jax.ShapeDtypeStruct((out_rows, D_MODEL), F32)]
    else:
        out_specs = [row(D_MODEL, 0), row(D_MODEL, 0)]
        out_shape = [jax.ShapeDtypeStruct((out_rows, D_MODEL), F32),
                     jax.ShapeDtypeStruct((out_rows, D_MODEL), BF16)]
    return pl.pallas_call(
        functools.partial(_merge_kernel, last=last),
        grid=(out_rows // tm,),
        in_specs=[row(WIDTH, 0), row(WIDTH, 0), row(WIDTH, 0),
                  row(WIDTH, C_SBZ // WIDTH), row(WIDTH, C_MZ // WIDTH), row(WIDTH, C_DZ // WIDTH),
                  row(D_MODEL, 0), row(D_MODEL, 1), row(D_MODEL, 2),
                  row(D_MODEL, 0), full(bg), full(w_sb), full(w_mla), full(w_diff), full(w_out), full(g_next)],
        out_specs=out_specs,
        out_shape=out_shape,
        compiler_params=_params(("parallel",)),
        name="merge_last" if last else "merge",
    )(o_sb, o_mla, o_diff, proj, proj, proj, proj, proj, proj, h, bg, w_sb, w_mla, w_diff, w_out, g_next)


def _qk(q, k):
    return lax.dot_general(q, k, (((1,), (1,)), ((), ())), preferred_element_type=F32)


def _iota2(shape, axis):
    return lax.broadcasted_iota(jnp.int32, shape, axis)


def _half_mask(rows, half):
    lane = _iota2((rows, LANES), 1)
    return (lane >= HEAD) if half else (lane < HEAD)


def _pv(p, v):
    return jnp.dot(p.astype(BF16), v, preferred_element_type=F32)


def _incl_tri(n):
    return (_iota2((n, n), 0) >= _iota2((n, n), 1)).astype(BF16)


def _sb_logw(z, tri):
    lk = jnp.minimum(-z, 0.0) - jnp.log(1.0 + jnp.exp(-jnp.abs(z)))
    hi = lk.astype(BF16)
    lo = (lk - hi.astype(F32)).astype(BF16)
    cum = jnp.dot(hi, tri, preferred_element_type=F32) + jnp.dot(lo, tri, preferred_element_type=F32)
    return z + cum, cum[:, 0:1]


def _wide(x, width):
    return jnp.concatenate([x] * (width // LANES), axis=-1)


def _row_blocks(t):
    return [pl.ds(r * ROW_CHUNK, ROW_CHUNK) for r in range(t // ROW_CHUNK)]


def _stream_cols(n_streams):
    return [slice((n // 2) * LANES, (n // 2 + 1) * LANES) for n in range(n_streams)]


def _half_queries(q_ref, t, scale):
    qs = []
    for n, c in enumerate(_stream_cols(2 * (q_ref.shape[1] // LANES))):
        qg = q_ref[:, c] if scale is None else q_ref[:, c] * scale
        qs.append(jnp.where(_half_mask(t, n % 2), qg, jnp.zeros_like(qg)))
    return qs


def _join_halves(o_ref, outs, t):
    for g in range(len(outs) // 2):
        o_ref[:, g * LANES:(g + 1) * LANES] = jnp.where(_half_mask(t, 0), outs[2 * g], outs[2 * g + 1]).astype(o_ref.dtype)


def _sb_kernel(q_ref, k_ref, v_ref, km_ref, vm_ref, o_ref, z_scr, w_scr, acc_scr, run_scr, mp_scr, *, n_tok_tiles):
    i = pl.program_id(2)
    t = ATT_TILE
    qs = _half_queries(q_ref, t, 0.125)
    n_s = len(qs)
    cols = _stream_cols(n_s)
    tri = _incl_tri(t)
    tri_m = _incl_tri(N_META)
    kms = [km_ref[:, c] for c in cols]
    vms = [vm_ref[:, c] for c in cols]
    blocks = _row_blocks(t)

    def tiles_at(ref, p):
        rows = pl.ds(pl.multiple_of(jnp.maximum(i - p, 0) * t, t), t)
        per_group = [ref[rows, cols[2 * g]] for g in range(n_s // 2)]
        return [per_group[n // 2] for n in range(n_s)]

    def scores(p):
        return [_qk(qh, kb) for qh, kb in zip(qs, tiles_at(k_ref, p))]

    def fold(vbs):
        loaded = [(n, r, w_scr[n, r, :], acc_scr[n, r, :]) for n in range(n_s) for r in blocks]
        return [(acc_scr, n, r, acc + _pv(jnp.exp(logw), vbs[n])) for n, r, logw, acc in loaded]

    def advance(zs, runs, masks):
        out = []
        for n in range(n_s):
            for b, r in enumerate(blocks):
                z, run = zs[n][b], runs[n][b]
                if masks is not None:
                    z = jnp.where(masks[b], z, NEG_BIG)
                part, total = _sb_logw(z, tri)
                out += [(w_scr, n, r, part + _wide(run, t)), (run_scr, n, r, run + total)]
        return out

    def store(writes):
        for ref, n, r, val in writes:
            ref[n, r, :] = val

    def tok():
        z0, z1 = scores(0), scores(1)
        zm = [_qk(qh, km) for qh, km in zip(qs, kms)]
        causal = [_iota2((ROW_CHUNK, t), 1) < _iota2((ROW_CHUNK, t), 0) + b * ROW_CHUNK for b in range(len(blocks))]
        acc_scr[...] = jnp.zeros_like(acc_scr)
        zero_run = jnp.zeros((ROW_CHUNK, LANES), F32)
        z0_blocks = [[z[b * ROW_CHUNK:(b + 1) * ROW_CHUNK] for b in range(len(blocks))] for z in z0]
        first = advance(z0_blocks, [[zero_run] * len(blocks)] * n_s, causal)
        store(first)
        for n in range(n_s):
            z_scr[n] = z1[n]
            mp_scr[n] = _sb_logw(zm[n], tri_m)[0]

        def largest_run(writes):
            runs = [val for ref, _, _, val in writes if ref is run_scr]
            return jnp.max(functools.reduce(jnp.maximum, runs))

        def alive(c):
            return jnp.logical_and(c[0] < i, c[1] >= SB_DEAD_RUN)

        def body(c):
            p = c[0]
            z_next = scores(p + 2)
            zs = [[z_scr[n, r, :] for r in blocks] for n in range(n_s)]
            runs = [[run_scr[n, r, :] for r in blocks] for n in range(n_s)]
            adv = advance(zs, runs, None)
            store(fold(tiles_at(v_ref, p)) + adv)
            for n in range(n_s):
                z_scr[n] = z_next[n]
            return p + 1, largest_run(adv)

        p_end, _ = lax.while_loop(alive, body, (jnp.int32(0), largest_run(first)))
        store(fold(tiles_at(v_ref, p_end)))
        return [acc_scr[n] + _pv(jnp.exp(mp_scr[n] + run_scr[n][:, :N_META]), vms[n]) for n in range(n_s)]

    def meta():
        causal_m = _iota2((t, N_META), 1) < _iota2((t, N_META), 0)
        zm = [_qk(qh, km) for qh, km in zip(qs, kms)]
        return [_pv(jnp.exp(_sb_logw(jnp.where(causal_m, z, NEG_BIG), tri_m)[0]), vm) for z, vm in zip(zm, vms)]

    _join_halves(o_ref, lax.cond(i < n_tok_tiles, tok, meta), t)


def _sb_scratch(n_s):
    t = ATT_TILE
    return [pltpu.VMEM((n_s, t, t), F32), pltpu.VMEM((n_s, t, t), F32), pltpu.VMEM((n_s, t, LANES), F32),
            pltpu.VMEM((n_s, t, LANES), F32), pltpu.VMEM((n_s, t, N_META), F32)]


def _row_max(s):
    return jnp.max(s, axis=-1, keepdims=True)


def _row_sum(p):
    return jnp.sum(p, axis=-1, keepdims=True)


def _softmax_attend(i, n_tok_tiles, qs, k_tile, v_tile, kms, vms, acc_scr, m_scr, l_scr):
    t = ATT_TILE
    n_s = len(qs)
    blocks = _row_blocks(t)

    def pipelined(issue, fold):
        pending = issue(0)
        for n in range(n_s):
            current, pending = pending, (issue(n + 1) if n + 1 < n_s else None)
            fold(n, current)

    def sweep(start, width):
        def fold(n, s):
            vb = v_tile(n, start, width)
            for b, r in enumerate(blocks):
                sb = s[b * ROW_CHUNK:(b + 1) * ROW_CHUNK]
                m_old = m_scr[n, r, :]
                m_new = jnp.maximum(m_old, _row_max(sb))
                alpha = jnp.exp2(m_old - m_new)
                p = jnp.exp2(sb - _wide(m_new, width))
                m_scr[n, r, :] = m_new
                l_scr[n, r, :] = alpha * l_scr[n, r, :] + _row_sum(p)
                acc_scr[n, r, :] = alpha * acc_scr[n, r, :] + _pv(p, vb)

        pipelined(lambda n: _qk(qs[n], k_tile(n, start, width)), fold)

    def tok():
        diag = pl.multiple_of(i * t, t)

        def first(n, scores):
            sd_all, sm_all = scores
            vd = v_tile(n, diag, t)
            for b, r in enumerate(blocks):
                rows = slice(b * ROW_CHUNK, (b + 1) * ROW_CHUNK)
                q_chunk = _iota2((ROW_CHUNK, t), 0) // CHUNK + b * (ROW_CHUNK // CHUNK)
                sd = jnp.where(_iota2((ROW_CHUNK, t), 1) // CHUNK <= q_chunk, sd_all[rows], NEG_BIG)
                sm = sm_all[rows]
                m = jnp.maximum(_row_max(sd), _row_max(sm))
                pd, pm = jnp.exp2(sd - m), jnp.exp2(sm - m)
                m_scr[n, r, :] = jnp.broadcast_to(m, (ROW_CHUNK, LANES))
                l_scr[n, r, :] = jnp.broadcast_to(_row_sum(pd) + _row_sum(pm), (ROW_CHUNK, LANES))
                acc_scr[n, r, :] = _pv(pd, vd) + _pv(pm, vms[n])

        pipelined(lambda n: (_qk(qs[n], k_tile(n, diag, t)), _qk(qs[n], kms[n])), first)

        def wide_tile(j, carry):
            sweep(pl.multiple_of(j * 2 * t, 2 * t), 2 * t)
            return carry

        lax.fori_loop(0, i // 2, wide_tile, 0)

        @pl.when(i % 2 == 1)
        def _():
            sweep(pl.multiple_of((i - 1) * t, t), t)

        return [acc_scr[n] * (1.0 / l_scr[n]) for n in range(n_s)]

    def meta():
        sms = [_qk(q, km) for q, km in zip(qs, kms)]
        pms = [jnp.exp2(sm - _row_max(sm)) for sm in sms]
        return [_pv(pm, vm) * (1.0 / _row_sum(pm)) for pm, vm in zip(pms, vms)]

    return lax.cond(i < n_tok_tiles, tok, meta)


def _softmax_scratch(n_s):
    t = ATT_TILE
    return [pltpu.VMEM((n_s, t, LANES), F32), pltpu.VMEM((n_s, t, LANES), F32), pltpu.VMEM((n_s, t, LANES), F32)]


def _mla_kernel(q_ref, k_ref, v_ref, km_ref, vm_ref, o_ref, *scratch, n_tok_tiles):
    i = pl.program_id(2)
    n_s = q_ref.shape[1] // LANES
    qcols = [slice(n * LANES, (n + 1) * LANES) for n in range(n_s)]
    vcols = _stream_cols(n_s)
    outs = _softmax_attend(
        i, n_tok_tiles, [q_ref[:, c] for c in qcols],
        lambda n, st, w: k_ref[pl.ds(st, w), qcols[n]], lambda n, st, w: v_ref[pl.ds(st, w), vcols[n]],
        [km_ref[:, c] for c in qcols], [vm_ref[:, c] for c in vcols], *scratch)
    _join_halves(o_ref, outs, ATT_TILE)


def _diff_kernel(q_ref, k_ref, v_ref, km_ref, vm_ref, lam_ref, g_ref, o_ref, *scratch, n_tok_tiles, lam_init):
    i = pl.program_id(2)
    qs = _half_queries(q_ref, ATT_TILE, None)
    cols = _stream_cols(len(qs))
    outs = _softmax_attend(
        i, n_tok_tiles, qs,
        lambda n, st, w: k_ref[pl.ds(st, w), cols[n]], lambda n, st, w: v_ref[pl.ds(st, w), cols[n]],
        [km_ref[:, c] for c in cols], [vm_ref[:, c] for c in cols], *scratch)
    lam4 = lam_ref[...]
    lam = (jnp.exp(jnp.sum(lam4[0:1] * lam4[1:2], axis=-1, keepdims=True))
           - jnp.exp(jnp.sum(lam4[2:3] * lam4[3:4], axis=-1, keepdims=True)) + lam_init)
    for g in range(len(qs) // 2):
        o = outs[2 * g] - lam * outs[2 * g + 1]
        o_ref[:, g * LANES:(g + 1) * LANES] = (_rms(o, g_ref[...]) * (1.0 - lam_init)).astype(o_ref.dtype)


def _attn_call(kernel, name, q, qw, qc, k, kw, kc, v, vc, extra, scratch, batch, seq, groups):
    rows = q.shape[0]
    t = ATT_TILE
    n_tok_tiles = seq // t
    tok_rows = batch * seq
    meta_q = tok_rows // t
    meta_k = tok_rows // N_META
    vw = WIDTH // groups

    def q_map(b, g, i):
        return (jnp.where(i < n_tok_tiles, b * n_tok_tiles + i, meta_q), qc + g)

    def o_map(b, g, i):
        return (jnp.where(i < n_tok_tiles, b * n_tok_tiles + i, meta_q + b), g)

    in_specs = [
        pl.BlockSpec((t, qw), q_map),
        pl.BlockSpec((seq, kw), lambda b, g, i: (b, kc + g)),
        pl.BlockSpec((seq, vw), lambda b, g, i: (b, vc + g)),
        pl.BlockSpec((N_META, kw), lambda b, g, i: (meta_k, kc + g)),
        pl.BlockSpec((N_META, vw), lambda b, g, i: (meta_k, vc + g)),
    ] + [pl.BlockSpec(a.shape, lambda b, g, i: (0, 0)) for a in extra]
    return pl.pallas_call(
        functools.partial(kernel, n_tok_tiles=n_tok_tiles),
        grid=(batch, groups, n_tok_tiles + 1),
        in_specs=in_specs,
        out_specs=pl.BlockSpec((t, vw), o_map),
        out_shape=jax.ShapeDtypeStruct((rows + (batch - 1) * t, WIDTH), BF16),
        scratch_shapes=scratch,
        compiler_params=_params(("parallel", "parallel", "arbitrary")),
        name=name,
    )(q, k, v, k, v, *extra)


def _perm_w_in(w_in):
    sizes = [512, 512, 512, 512, 384, 256, 32, 512, 512, 512, 512, 512, 3072]
    offs = [0]
    for s in sizes:
        offs.append(offs[-1] + s)
    seg = [w_in[..., offs[n]:offs[n + 1]] for n in range(len(sizes))]
    sb_q, sb_k, sb_v, sb_z, cq, ckv, kr, mz, dq, dk, dv, dz, gate = seg
    pad = lambda n: jnp.zeros(w_in.shape[:-1] + (n,), w_in.dtype)
    cols = [gate, sb_q, sb_k, sb_v, sb_z, dq, dk, dv, dz, mz, ckv, cq, pad(HEAD), kr, pad(LANES - HEAD - MLA_ROPE)]
    return jnp.concatenate(cols, axis=-1).astype(BF16)


def _rope_tables(seq):
    pos = jnp.concatenate([jnp.arange(seq) + N_META, jnp.arange(META_ROWS)]).astype(F32)[:, None]

    def tables(dim, starts):
        inv = ROPE_THETA ** (-jnp.arange(0, dim, 2, dtype=F32) / dim)
        ang = pos * inv[None, :]
        cos, sin = jnp.cos(ang), jnp.sin(ang)
        half = dim // 2
        c = jnp.ones((pos.shape[0], LANES), F32)
        s1 = jnp.zeros((pos.shape[0], LANES), F32)
        s2 = jnp.zeros((pos.shape[0], LANES), F32)
        for st in starts:
            c = c.at[:, st:st + half].set(cos).at[:, st + half:st + dim].set(cos)
            s1 = s1.at[:, st:st + half].set(-sin)
            s2 = s2.at[:, st + half:st + dim].set(sin)
        return [c, s1, s2]

    return tables(MLA_ROPE, [HEAD]) + tables(DIFF_ROT, [0, HEAD])


def _pick(rows, candidates):
    for c in candidates:
        if rows % c == 0:
            return c
    raise ValueError(f"no row tile for {rows}")


def kernel(x, meta_tokens, norm_g, w_in, b_gate, mla_cq_g, mla_ckv_g, mla_w_uq, mla_w_ukv, diff_lambda,
           diff_norm_g, w_o_sb, w_o_mla, w_o_diff, w_out, final_g):
    batch, seq, d = x.shape
    depth = norm_g.shape[0]
    assert d == D_MODEL and seq % ATT_TILE == 0 and meta_tokens.shape == (N_META, D_MODEL)
    tok_rows = batch * seq
    rows = tok_rows + META_ROWS
    tm = ROW_TILE
    tm_proj = _pick(rows, (1280, 1024, 768, 512, 256))

    h = jnp.concatenate([x.reshape(tok_rows, d), meta_tokens.astype(x.dtype),
                         jnp.zeros((META_ROWS - N_META, d), x.dtype)], axis=0)
    w_in_p = _perm_w_in(w_in)
    wuq = jnp.pad(mla_w_uq.reshape(depth, MLA_Q_RANK, MLA_HEADS, HEAD + MLA_ROPE),
                  ((0, 0), (0, 0), (0, 0), (0, LANES - HEAD - MLA_ROPE))).reshape(depth, MLA_Q_RANK, -1).astype(BF16)
    wukv = mla_w_ukv.reshape(depth, MLA_KV_RANK, MLA_HEADS, 2 * HEAD)
    wuk = jnp.pad(wukv[..., :HEAD], ((0, 0), (0, 0), (0, 0), (0, LANES - HEAD))).reshape(depth, MLA_KV_RANK, -1).astype(BF16)
    wuv = wukv[..., HEAD:].reshape(depth, MLA_KV_RANK, -1).astype(BF16)
    tabs = _rope_tables(seq)
    row2 = lambda a: a.reshape(1, -1)

    hn = _norm_call(h, row2(norm_g[0]), tm)
    out = None
    for l in range(depth):
        last = l == depth - 1
        proj = _proj_call(hn, w_in_p[l], tm_proj)
        q_m, k_m, v_m, q_d, k_d = _prep_call(proj, row2(mla_cq_g[l]), row2(mla_ckv_g[l]), wuq[l], wuk[l], wuv[l],
                                             tabs, tm, tok_rows, seq)
        gw = ATT_GROUPS * LANES
        n_s = 2 * ATT_GROUPS
        o_sb = _attn_call(_sb_kernel, "sb_attn", proj, gw, C_SBQ // gw, proj, gw, C_SBK // gw,
                          proj, C_SBV // gw, [], _sb_scratch(n_s), batch, seq, WIDTH // gw)
        o_mla = _attn_call(_mla_kernel, "mla_attn", q_m, 2 * gw, 0, k_m, 2 * gw, 0, v_m, 0, [],
                           _softmax_scratch(n_s), batch, seq, WIDTH // gw)
        lam_init = 0.8 - 0.6 * math.exp(-0.3 * l)
        o_diff = _attn_call(functools.partial(_diff_kernel, lam_init=lam_init), "diff_attn",
                            q_d, gw, 0, k_d, gw, 0, proj, C_DV // gw,
                            [diff_lambda[l].astype(F32), row2(diff_norm_g[l])], _softmax_scratch(n_s), batch, seq,
                            WIDTH // gw)
        g_next = row2(final_g if last else norm_g[l + 1])
        res = _merge_call(o_sb, o_mla, o_diff, proj, h, row2(b_gate[l]), w_o_sb[l].astype(BF16),
                          w_o_mla[l].astype(BF16), w_o_diff[l].astype(BF16), w_out[l].astype(BF16), g_next,
                          tm, tok_rows if last else rows, last)
        if last:
            out = res[0]
        else:
            h, hn = res
    return out.reshape(batch, seq, d)
```

```python
import functools
import math

import jax
import jax.numpy as jnp
import numpy as np
from jax import lax
from jax.experimental import pallas as pl
from jax.experimental.pallas import tpu as pltpu

F32 = jnp.float32
BF16 = jnp.bfloat16

D_MODEL = 1024
CHUNK = 64
N_META = 16
ROPE_THETA = 500000.0
EPS = 1e-6
LANES = 128
HEAD = 64
WIDTH = 512
MLA_HEADS = 8
MLA_ROPE = 32
MLA_Q_RANK = 384
MLA_KV_RANK = 256
DIFF_ROT = 16

META_ROWS = 256
ROW_TILE = 256
ATT_TILE = 256
ROW_CHUNK = 128
ATT_GROUPS = 4
NEG_BIG = -1e30
LOG2E = 1.4426950408889634
SB_DEAD_RUN = -104.0
VMEM_LIMIT = 56 * 1024 * 1024

C_GATE = 0
C_SBQ = 3072
C_SBK = 3584
C_SBV = 4096
C_SBZ = 4608
C_DQ = 5120
C_DK = 5632
C_DV = 6144
C_DZ = 6656
C_MZ = 7168
C_LAT = 7680
LAT_W = 768
N_PROJ = 8448
PROJ_TN = 768


def _params(sem):
    return pltpu.CompilerParams(dimension_semantics=sem, vmem_limit_bytes=VMEM_LIMIT)


def _rms(x32, g):
    ms = jnp.mean(x32 * x32, axis=-1, keepdims=True)
    return x32 * lax.rsqrt(ms + EPS) * g


def _residual_rows(tok_ref, meta_ref, n_tok_tiles):
    return jnp.where(pl.program_id(0) < n_tok_tiles, tok_ref[...], meta_ref[...])


def _residual_specs(tm, n_tok_tiles, meta_block):
    return [pl.BlockSpec((tm, D_MODEL), lambda i: (jnp.minimum(i, n_tok_tiles - 1), 0)),
            pl.BlockSpec((tm, D_MODEL), lambda i: (meta_block, 0))]


def _norm_kernel(tok_ref, meta_ref, g_ref, o_ref, *, n_tok_tiles):
    o_ref[...] = _rms(_residual_rows(tok_ref, meta_ref, n_tok_tiles), g_ref[...]).astype(o_ref.dtype)


def _norm_call(h_tok, h_meta, g, tm):
    n_tok_tiles = h_tok.shape[0] // tm
    rows = h_tok.shape[0] + h_meta.shape[0]
    return pl.pallas_call(
        functools.partial(_norm_kernel, n_tok_tiles=n_tok_tiles),
        grid=(rows // tm,),
        in_specs=_residual_specs(tm, n_tok_tiles, 0) + [pl.BlockSpec((1, D_MODEL), lambda i: (0, 0))],
        out_specs=pl.BlockSpec((tm, D_MODEL), lambda i: (i, 0)),
        out_shape=jax.ShapeDtypeStruct((rows, D_MODEL), BF16),
        compiler_params=_params(("parallel",)),
        name="norm_in",
    )(h_tok, h_meta, g)


def _proj_kernel(a_ref, w_ref, o_ref):
    o_ref[...] = jnp.dot(a_ref[...], w_ref[...], preferred_element_type=F32).astype(o_ref.dtype)


def _proj_call(hn, w, tm):
    rows = hn.shape[0]
    return pl.pallas_call(
        _proj_kernel,
        grid=(rows // tm, N_PROJ // PROJ_TN),
        in_specs=[pl.BlockSpec((tm, D_MODEL), lambda i, j: (i, 0)),
                  pl.BlockSpec((D_MODEL, PROJ_TN), lambda i, j: (0, j))],
        out_specs=pl.BlockSpec((tm, PROJ_TN), lambda i, j: (i, j)),
        out_shape=jax.ShapeDtypeStruct((rows, N_PROJ), BF16),
        compiler_params=_params(("parallel", "arbitrary")),
        name="in_proj",
    )(hn, w)


def _rot(x, c, s1, s2, shift):
    return x * c + pltpu.roll(x, LANES - shift, 1) * s1 + pltpu.roll(x, shift, 1) * s2


def _prep_kernel(lat_ref, dq_ref, dk_ref, gq_ref, gkv_ref, wuq_ref, wuk_ref, wuv_ref,
                 cm_ref, s1m_ref, s2m_ref, cd_ref, s1d_ref, s2d_ref,
                 qm_ref, km_ref, vm_ref, dqo_ref, dko_ref):
    lat = lat_ref[...].astype(F32)
    ckv = lat[:, :MLA_KV_RANK]
    cq = lat[:, MLA_KV_RANK:MLA_KV_RANK + MLA_Q_RANK]
    kr = lat[:, MLA_KV_RANK + MLA_Q_RANK:]
    ncq = _rms(cq, gq_ref[...]).astype(BF16)
    nckv = _rms(ckv, gkv_ref[...]).astype(BF16)
    cm, s1m, s2m = cm_ref[...], s1m_ref[...], s2m_ref[...]
    scale_b = LOG2E / math.sqrt(HEAD + MLA_ROPE)
    k_rope = _rot(kr, cm, s1m, s2m, MLA_ROPE // 2)
    vm_ref[...] = jnp.dot(nckv, wuv_ref[...], preferred_element_type=F32).astype(BF16)
    for h in range(MLA_HEADS):
        sl = slice(h * LANES, (h + 1) * LANES)
        qf = jnp.dot(ncq, wuq_ref[:, sl], preferred_element_type=F32)
        qm_ref[:, sl] = (_rot(qf, cm, s1m, s2m, MLA_ROPE // 2) * scale_b).astype(BF16)
        kf = jnp.dot(nckv, wuk_ref[:, sl], preferred_element_type=F32)
        km_ref[:, sl] = (kf + k_rope).astype(BF16)
    cd, s1d, s2d = cd_ref[...], s1d_ref[...], s2d_ref[...]
    for h in range(WIDTH // LANES):
        sl = slice(h * LANES, (h + 1) * LANES)
        dqo_ref[:, sl] = (_rot(dq_ref[:, sl].astype(F32), cd, s1d, s2d, DIFF_ROT // 2) * (LOG2E / math.sqrt(HEAD))).astype(BF16)
        dko_ref[:, sl] = _rot(dk_ref[:, sl].astype(F32), cd, s1d, s2d, DIFF_ROT // 2).astype(BF16)


def _prep_call(proj, gq, gkv, wuq, wuk, wuv, tabs, tm, tok_rows, seq):
    rows = proj.shape[0]
    n_tok_tiles = tok_rows // tm
    per_seq = seq // tm

    def tab_map(i):
        return (jnp.where(i < n_tok_tiles, i % per_seq, per_seq), 0)

    row = lambda w, c: pl.BlockSpec((tm, w), lambda i: (i, c))
    full = lambda a: pl.BlockSpec(a.shape, lambda i: (0, 0))
    tab = pl.BlockSpec((tm, LANES), tab_map)
    out = lambda w: jax.ShapeDtypeStruct((rows, w), BF16)
    return pl.pallas_call(
        _prep_kernel,
        grid=(rows // tm,),
        in_specs=[row(LAT_W, C_LAT // LAT_W), row(WIDTH, C_DQ // WIDTH), row(WIDTH, C_DK // WIDTH),
                  full(gq), full(gkv), full(wuq), full(wuk), full(wuv)] + [tab] * 6,
        out_specs=[row(MLA_HEADS * LANES, 0), row(MLA_HEADS * LANES, 0), row(WIDTH, 0),
                   row(WIDTH, 0), row(WIDTH, 0)],
        out_shape=[out(MLA_HEADS * LANES), out(MLA_HEADS * LANES), out(WIDTH), out(WIDTH), out(WIDTH)],
        compiler_params=_params(("parallel",)),
        name="prep",
    )(proj, proj, proj, gq, gkv, wuq, wuk, wuv, *tabs)


def _sigmoid(x):
    return 0.5 * jnp.tanh(0.5 * x) + 0.5


def _merge_kernel(osb_ref, omla_ref, odiff_ref, zsb_ref, zmla_ref, zdiff_ref, g0_ref, g1_ref, g2_ref,
                  htok_ref, hmeta_ref, bg_ref, wsb_ref, wmla_ref, wdiff_ref, wout_ref, gn_ref, *out_refs,
                  last, n_tok_tiles):
    def branch(o_ref, z_ref, w_ref):
        z = z_ref[...].astype(F32)
        a = o_ref[...].astype(F32) * (z * _sigmoid(z))
        return jnp.dot(a.astype(BF16), w_ref[...], preferred_element_type=F32)

    bg = bg_ref[...]
    merged = _sigmoid(g0_ref[...].astype(F32) + bg[:, :D_MODEL]) * branch(osb_ref, zsb_ref, wsb_ref)
    merged += _sigmoid(g1_ref[...].astype(F32) + bg[:, D_MODEL:2 * D_MODEL]) * branch(omla_ref, zmla_ref, wmla_ref)
    merged += _sigmoid(g2_ref[...].astype(F32) + bg[:, 2 * D_MODEL:]) * branch(odiff_ref, zdiff_ref, wdiff_ref)
    h_old = _residual_rows(htok_ref, hmeta_ref, n_tok_tiles)
    h_new = h_old + jnp.dot(merged.astype(BF16), wout_ref[...], preferred_element_type=F32)
    normed = _rms(h_new, gn_ref[...])
    if last:
        out_refs[0][...] = normed
    else:
        out_refs[0][...] = h_new
        out_refs[1][...] = normed.astype(BF16)


def _merge_call(o_sb, o_mla, o_diff, proj, h_tok, h_meta, meta_block, n_tok_tiles, bg, w_sb, w_mla, w_diff, w_out,
                g_next, tm, out_rows, last):
    row = lambda w, c: pl.BlockSpec((tm, w), lambda i: (i, c))
    full = lambda a: pl.BlockSpec(a.shape, lambda i: (0, 0))
    if last:
        out_specs = [row(D_MODEL, 0)]
        out_shape = [jax.ShapeDtypeStruct((out_rows, D_MODEL), F32)]
    else:
        out_specs = [row(D_MODEL, 0), row(D_MODEL, 0)]
        out_shape = [jax.ShapeDtypeStruct((out_rows, D_MODEL), F32),
                     jax.ShapeDtypeStruct((out_rows, D_MODEL), BF16)]
    return pl.pallas_call(
        functools.partial(_merge_kernel, last=last, n_tok_tiles=n_tok_tiles),
        grid=(out_rows // tm,),
        in_specs=[row(WIDTH, 0), row(WIDTH, 0), row(WIDTH, 0),
                  row(WIDTH, C_SBZ // WIDTH), row(WIDTH, C_MZ // WIDTH), row(WIDTH, C_DZ // WIDTH),
                  row(D_MODEL, 0), row(D_MODEL, 1), row(D_MODEL, 2)]
                 + _residual_specs(tm, n_tok_tiles, meta_block)
                 + [full(bg), full(w_sb), full(w_mla), full(w_diff), full(w_out), full(g_next)],
        out_specs=out_specs,
        out_shape=out_shape,
        compiler_params=_params(("parallel",)),
        name="merge_last" if last else "merge",
    )(o_sb, o_mla, o_diff, proj, proj, proj, proj, proj, proj, h_tok, h_meta, bg, w_sb, w_mla, w_diff, w_out, g_next)


def _qk(q, k):
    return lax.dot_general(q, k, (((1,), (1,)), ((), ())), preferred_element_type=F32)


def _iota2(shape, axis):
    return lax.broadcasted_iota(jnp.int32, shape, axis)


def _half_mask(rows, half):
    lane = _iota2((rows, LANES), 1)
    return (lane >= HEAD) if half else (lane < HEAD)


def _pv(p, v):
    return jnp.dot(p.astype(BF16), v, preferred_element_type=F32)


def _incl_tri(n):
    return (_iota2((n, n), 0) >= _iota2((n, n), 1)).astype(BF16)


def _sb_logw(z, tri):
    lk = jnp.minimum(-z, 0.0) - jnp.log(1.0 + jnp.exp(-jnp.abs(z)))
    hi = lk.astype(BF16)
    lo = (lk - hi.astype(F32)).astype(BF16)
    cum = jnp.dot(hi, tri, preferred_element_type=F32) + jnp.dot(lo, tri, preferred_element_type=F32)
    return z + cum, cum[:, 0:1]


def _wide(x, width):
    return jnp.concatenate([x] * (width // LANES), axis=-1)


def _row_blocks(t):
    return [pl.ds(r * ROW_CHUNK, ROW_CHUNK) for r in range(t // ROW_CHUNK)]


def _stream_cols(n_streams):
    return [slice((n // 2) * LANES, (n // 2 + 1) * LANES) for n in range(n_streams)]


def _half_queries(q_ref, t, scale):
    qs = []
    for n, c in enumerate(_stream_cols(2 * (q_ref.shape[1] // LANES))):
        qg = q_ref[:, c] if scale is None else q_ref[:, c] * scale
        qs.append(jnp.where(_half_mask(t, n % 2), qg, jnp.zeros_like(qg)))
    return qs


def _join_halves(o_ref, outs, t):
    for g in range(len(outs) // 2):
        o_ref[:, g * LANES:(g + 1) * LANES] = jnp.where(_half_mask(t, 0), outs[2 * g], outs[2 * g + 1]).astype(o_ref.dtype)


def _sb_kernel(q_ref, k_ref, v_ref, km_ref, vm_ref, o_ref, z_scr, w_scr, acc_scr, run_scr, mp_scr, *, n_tok_tiles):
    i = pl.program_id(2)
    t = ATT_TILE
    qs = _half_queries(q_ref, t, 0.125)
    n_s = len(qs)
    cols = _stream_cols(n_s)
    tri = _incl_tri(t)
    tri_m = _incl_tri(N_META)
    kms = [km_ref[:, c] for c in cols]
    vms = [vm_ref[:, c] for c in cols]
    blocks = _row_blocks(t)

    def tiles_at(ref, p):
        rows = pl.ds(pl.multiple_of(jnp.maximum(i - p, 0) * t, t), t)
        per_group = [ref[rows, cols[2 * g]] for g in range(n_s // 2)]
        return [per_group[n // 2] for n in range(n_s)]

    def scores(p):
        return [_qk(qh, kb) for qh, kb in zip(qs, tiles_at(k_ref, p))]

    def fold(vbs):
        loaded = [(n, r, w_scr[n, r, :], acc_scr[n, r, :]) for n in range(n_s) for r in blocks]
        return [(acc_scr, n, r, acc + _pv(jnp.exp(logw), vbs[n])) for n, r, logw, acc in loaded]

    def advance(zs, runs, masks):
        out = []
        for n in range(n_s):
            for b, r in enumerate(blocks):
                z, run = zs[n][b], runs[n][b]
                if masks is not None:
                    z = jnp.where(masks[b], z, NEG_BIG)
                part, total = _sb_logw(z, tri)
                out += [(w_scr, n, r, part + _wide(run, t)), (run_scr, n, r, run + total)]
        return out

    def store(writes):
        for ref, n, r, val in writes:
            ref[n, r, :] = val

    def tok():
        z0, z1 = scores(0), scores(1)
        zm = [_qk(qh, km) for qh, km in zip(qs, kms)]
        causal = [_iota2((ROW_CHUNK, t), 1) < _iota2((ROW_CHUNK, t), 0) + b * ROW_CHUNK for b in range(len(blocks))]
        acc_scr[...] = jnp.zeros_like(acc_scr)
        zero_run = jnp.zeros((ROW_CHUNK, LANES), F32)
        z0_blocks = [[z[b * ROW_CHUNK:(b + 1) * ROW_CHUNK] for b in range(len(blocks))] for z in z0]
        first = advance(z0_blocks, [[zero_run] * len(blocks)] * n_s, causal)
        store(first)
        for n in range(n_s):
            z_scr[n] = z1[n]
            mp_scr[n] = _sb_logw(zm[n], tri_m)[0]

        def largest_run(writes):
            runs = [val for ref, _, _, val in writes if ref is run_scr]
            return jnp.max(functools.reduce(jnp.maximum, runs))

        def alive(c):
            return jnp.logical_and(c[0] < i, c[1] >= SB_DEAD_RUN)

        def body(c):
            p = c[0]
            z_next = scores(p + 2)
            zs = [[z_scr[n, r, :] for r in blocks] for n in range(n_s)]
            runs = [[run_scr[n, r, :] for r in blocks] for n in range(n_s)]
            adv = advance(zs, runs, None)
            store(fold(tiles_at(v_ref, p)) + adv)
            for n in range(n_s):
                z_scr[n] = z_next[n]
            return p + 1, largest_run(adv)

        p_end, _ = lax.while_loop(alive, body, (jnp.int32(0), largest_run(first)))
        store(fold(tiles_at(v_ref, p_end)))
        return [acc_scr[n] + _pv(jnp.exp(mp_scr[n] + run_scr[n][:, :N_META]), vms[n]) for n in range(n_s)]

    def meta():
        causal_m = _iota2((t, N_META), 1) < _iota2((t, N_META), 0)
        zm = [_qk(qh, km) for qh, km in zip(qs, kms)]
        return [_pv(jnp.exp(_sb_logw(jnp.where(causal_m, z, NEG_BIG), tri_m)[0]), vm) for z, vm in zip(zm, vms)]

    _join_halves(o_ref, lax.cond(i < n_tok_tiles, tok, meta), t)


def _sb_scratch(n_s):
    t = ATT_TILE
    return [pltpu.VMEM((n_s, t, t), F32), pltpu.VMEM((n_s, t, t), F32), pltpu.VMEM((n_s, t, LANES), F32),
            pltpu.VMEM((n_s, t, LANES), F32), pltpu.VMEM((n_s, t, N_META), F32)]


def _row_max(s):
    return jnp.max(s, axis=-1, keepdims=True)


def _row_sum(p):
    return jnp.sum(p, axis=-1, keepdims=True)


def _softmax_attend(i, n_tok_tiles, qs, k_tile, v_tile, kms, vms, acc_scr, m_scr, l_scr):
    t = ATT_TILE
    n_s = len(qs)
    blocks = _row_blocks(t)

    def pipelined(issue, fold):
        pending = issue(0)
        for n in range(n_s):
            current, pending = pending, (issue(n + 1) if n + 1 < n_s else None)
            fold(n, current)

    def sweep(start, width):
        def fold(n, s):
            vb = v_tile(n, start, width)
            for b, r in enumerate(blocks):
                sb = s[b * ROW_CHUNK:(b + 1) * ROW_CHUNK]
                m_old = m_scr[n, r, :]
                m_new = jnp.maximum(m_old, _row_max(sb))
                alpha = jnp.exp2(m_old - m_new)
                p = jnp.exp2(sb - _wide(m_new, width))
                m_scr[n, r, :] = m_new
                l_scr[n, r, :] = alpha * l_scr[n, r, :] + _row_sum(p)
                acc_scr[n, r, :] = alpha * acc_scr[n, r, :] + _pv(p, vb)

        pipelined(lambda n: _qk(qs[n], k_tile(n, start, width)), fold)

    def tok():
        diag = pl.multiple_of(i * t, t)

        def first(n, scores):
            sd_all, sm_all = scores
            vd = v_tile(n, diag, t)
            for b, r in enumerate(blocks):
                rows = slice(b * ROW_CHUNK, (b + 1) * ROW_CHUNK)
                q_chunk = _iota2((ROW_CHUNK, t), 0) // CHUNK + b * (ROW_CHUNK // CHUNK)
                sd = jnp.where(_iota2((ROW_CHUNK, t), 1) // CHUNK <= q_chunk, sd_all[rows], NEG_BIG)
                sm = sm_all[rows]
                m = jnp.maximum(_row_max(sd), _row_max(sm))
                pd, pm = jnp.exp2(sd - m), jnp.exp2(sm - m)
                m_scr[n, r, :] = jnp.broadcast_to(m, (ROW_CHUNK, LANES))
                l_scr[n, r, :] = jnp.broadcast_to(_row_sum(pd) + _row_sum(pm), (ROW_CHUNK, LANES))
                acc_scr[n, r, :] = _pv(pd, vd) + _pv(pm, vms[n])

        pipelined(lambda n: (_qk(qs[n], k_tile(n, diag, t)), _qk(qs[n], kms[n])), first)

        def wide_tile(j, carry):
            sweep(pl.multiple_of(j * 2 * t, 2 * t), 2 * t)
            return carry

        lax.fori_loop(0, i // 2, wide_tile, 0)

        @pl.when(i % 2 == 1)
        def _():
            sweep(pl.multiple_of((i - 1) * t, t), t)

        return [acc_scr[n] * (1.0 / l_scr[n]) for n in range(n_s)]

    def meta():
        sms = [_qk(q, km) for q, km in zip(qs, kms)]
        pms = [jnp.exp2(sm - _row_max(sm)) for sm in sms]
        return [_pv(pm, vm) * (1.0 / _row_sum(pm)) for pm, vm in zip(pms, vms)]

    return lax.cond(i < n_tok_tiles, tok, meta)


def _softmax_scratch(n_s):
    t = ATT_TILE
    return [pltpu.VMEM((n_s, t, LANES), F32), pltpu.VMEM((n_s, t, LANES), F32), pltpu.VMEM((n_s, t, LANES), F32)]


def _mla_kernel(q_ref, k_ref, v_ref, km_ref, vm_ref, o_ref, *scratch, n_tok_tiles):
    i = pl.program_id(2)
    n_s = q_ref.shape[1] // LANES
    qcols = [slice(n * LANES, (n + 1) * LANES) for n in range(n_s)]
    vcols = _stream_cols(n_s)
    outs = _softmax_attend(
        i, n_tok_tiles, [q_ref[:, c] for c in qcols],
        lambda n, st, w: k_ref[pl.ds(st, w), qcols[n]], lambda n, st, w: v_ref[pl.ds(st, w), vcols[n]],
        [km_ref[:, c] for c in qcols], [vm_ref[:, c] for c in vcols], *scratch)
    _join_halves(o_ref, outs, ATT_TILE)


def _diff_kernel(q_ref, k_ref, v_ref, km_ref, vm_ref, lam_ref, g_ref, o_ref, *scratch, n_tok_tiles, lam_init):
    i = pl.program_id(2)
    qs = _half_queries(q_ref, ATT_TILE, None)
    cols = _stream_cols(len(qs))
    outs = _softmax_attend(
        i, n_tok_tiles, qs,
        lambda n, st, w: k_ref[pl.ds(st, w), cols[n]], lambda n, st, w: v_ref[pl.ds(st, w), cols[n]],
        [km_ref[:, c] for c in cols], [vm_ref[:, c] for c in cols], *scratch)
    lam4 = lam_ref[...]
    lam = (jnp.exp(jnp.sum(lam4[0:1] * lam4[1:2], axis=-1, keepdims=True))
           - jnp.exp(jnp.sum(lam4[2:3] * lam4[3:4], axis=-1, keepdims=True)) + lam_init)
    for g in range(len(qs) // 2):
        o = outs[2 * g] - lam * outs[2 * g + 1]
        o_ref[:, g * LANES:(g + 1) * LANES] = (_rms(o, g_ref[...]) * (1.0 - lam_init)).astype(o_ref.dtype)


def _attn_call(kernel, name, q, qw, qc, k, kw, kc, v, vc, extra, scratch, batch, seq, groups):
    rows = q.shape[0]
    t = ATT_TILE
    n_tok_tiles = seq // t
    tok_rows = batch * seq
    meta_q = tok_rows // t
    meta_k = tok_rows // N_META
    vw = WIDTH // groups

    def q_map(b, g, i):
        return (jnp.where(i < n_tok_tiles, b * n_tok_tiles + i, meta_q), qc + g)

    def o_map(b, g, i):
        return (jnp.where(i < n_tok_tiles, b * n_tok_tiles + i, meta_q + b), g)

    in_specs = [
        pl.BlockSpec((t, qw), q_map),
        pl.BlockSpec((seq, kw), lambda b, g, i: (b, kc + g)),
        pl.BlockSpec((seq, vw), lambda b, g, i: (b, vc + g)),
        pl.BlockSpec((N_META, kw), lambda b, g, i: (meta_k, kc + g)),
        pl.BlockSpec((N_META, vw), lambda b, g, i: (meta_k, vc + g)),
    ] + [pl.BlockSpec(a.shape, lambda b, g, i: (0, 0)) for a in extra]
    return pl.pallas_call(
        functools.partial(kernel, n_tok_tiles=n_tok_tiles),
        grid=(batch, groups, n_tok_tiles + 1),
        in_specs=in_specs,
        out_specs=pl.BlockSpec((t, vw), o_map),
        out_shape=jax.ShapeDtypeStruct((rows + (batch - 1) * t, WIDTH), BF16),
        scratch_shapes=scratch,
        compiler_params=_params(("parallel", "parallel", "arbitrary")),
        name=name,
    )(q, k, v, k, v, *extra)


def _perm_w_in(w_in):
    sizes = [512, 512, 512, 512, 384, 256, 32, 512, 512, 512, 512, 512, 3072]
    offs = [0]
    for s in sizes:
        offs.append(offs[-1] + s)
    seg = [w_in[..., offs[n]:offs[n + 1]] for n in range(len(sizes))]
    sb_q, sb_k, sb_v, sb_z, cq, ckv, kr, mz, dq, dk, dv, dz, gate = seg
    pad = lambda n: jnp.zeros(w_in.shape[:-1] + (n,), w_in.dtype)
    cols = [gate, sb_q, sb_k, sb_v, sb_z, dq, dk, dv, dz, mz, ckv, cq, pad(HEAD), kr, pad(LANES - HEAD - MLA_ROPE)]
    return jnp.concatenate(cols, axis=-1).astype(BF16)


def _rope_tables(seq):
    pos = np.concatenate([np.arange(seq) + N_META, np.arange(META_ROWS)]).astype(np.float64)[:, None]

    def tables(dim, starts):
        inv = ROPE_THETA ** (-np.arange(0, dim, 2, dtype=np.float64) / dim)
        ang = pos * inv[None, :]
        cos, sin = np.cos(ang), np.sin(ang)
        half = dim // 2
        c = np.ones((pos.shape[0], LANES))
        s1 = np.zeros((pos.shape[0], LANES))
        s2 = np.zeros((pos.shape[0], LANES))
        for st in starts:
            c[:, st:st + half] = cos
            c[:, st + half:st + dim] = cos
            s1[:, st:st + half] = -sin
            s2[:, st + half:st + dim] = sin
        return [jnp.asarray(a, F32) for a in (c, s1, s2)]

    return tables(MLA_ROPE, [HEAD]) + tables(DIFF_ROT, [0, HEAD])


def _pick(rows, candidates):
    for c in candidates:
        if rows % c == 0:
            return c
    raise ValueError(f"no row tile for {rows}")


def kernel(x, meta_tokens, norm_g, w_in, b_gate, mla_cq_g, mla_ckv_g, mla_w_uq, mla_w_ukv, diff_lambda,
           diff_norm_g, w_o_sb, w_o_mla, w_o_diff, w_out, final_g):
    batch, seq, d = x.shape
    depth = norm_g.shape[0]
    assert d == D_MODEL and seq % ATT_TILE == 0 and meta_tokens.shape == (N_META, D_MODEL)
    tok_rows = batch * seq
    rows = tok_rows + META_ROWS
    tm = ROW_TILE
    tm_proj = _pick(rows, (1280, 1024, 768, 512, 256))

    n_tok_tiles = tok_rows // tm
    h_tok = x.reshape(tok_rows, d)
    h_meta = jnp.concatenate([meta_tokens.astype(x.dtype), jnp.zeros((META_ROWS - N_META, d), x.dtype)], axis=0)
    meta_block = 0
    wuq = jnp.pad(mla_w_uq.reshape(depth, MLA_Q_RANK, MLA_HEADS, HEAD + MLA_ROPE),
                  ((0, 0), (0, 0), (0, 0), (0, LANES - HEAD - MLA_ROPE))).reshape(depth, MLA_Q_RANK, -1).astype(BF16)
    wukv = mla_w_ukv.reshape(depth, MLA_KV_RANK, MLA_HEADS, 2 * HEAD)
    wuk = jnp.pad(wukv[..., :HEAD], ((0, 0), (0, 0), (0, 0), (0, LANES - HEAD))).reshape(depth, MLA_KV_RANK, -1).astype(BF16)
    wuv = wukv[..., HEAD:].reshape(depth, MLA_KV_RANK, -1).astype(BF16)
    tabs = _rope_tables(seq)
    row2 = lambda a: a.reshape(1, -1)

    hn = _norm_call(h_tok, h_meta, row2(norm_g[0]), tm)
    out = None
    for l in range(depth):
        last = l == depth - 1
        proj = _proj_call(hn, _perm_w_in(w_in[l]), tm_proj)
        q_m, k_m, v_m, q_d, k_d = _prep_call(proj, row2(mla_cq_g[l]), row2(mla_ckv_g[l]), wuq[l], wuk[l], wuv[l],
                                             tabs, tm, tok_rows, seq)
        gw = ATT_GROUPS * LANES
        n_s = 2 * ATT_GROUPS
        o_sb = _attn_call(_sb_kernel, "sb_attn", proj, gw, C_SBQ // gw, proj, gw, C_SBK // gw,
                          proj, C_SBV // gw, [], _sb_scratch(n_s), batch, seq, WIDTH // gw)
        o_mla = _attn_call(_mla_kernel, "mla_attn", q_m, 2 * gw, 0, k_m, 2 * gw, 0, v_m, 0, [],
                           _softmax_scratch(n_s), batch, seq, WIDTH // gw)
        lam_init = 0.8 - 0.6 * math.exp(-0.3 * l)
        o_diff = _attn_call(functools.partial(_diff_kernel, lam_init=lam_init), "diff_attn",
                            q_d, gw, 0, k_d, gw, 0, proj, C_DV // gw,
                            [diff_lambda[l].astype(F32), row2(diff_norm_g[l])], _softmax_scratch(n_s), batch, seq,
                            WIDTH // gw)
        g_next = row2(final_g if last else norm_g[l + 1])
        res = _merge_call(o_sb, o_mla, o_diff, proj, h_tok, h_meta, meta_block, n_tok_tiles, row2(b_gate[l]),
                          w_o_sb[l].astype(BF16), w_o_mla[l].astype(BF16), w_o_diff[l].astype(BF16),
                          w_out[l].astype(BF16), g_next, tm, tok_rows if last else rows, last)
        if last:
            out = res[0]
        else:
            h, hn = res
            h_tok, h_meta, meta_block = h, h, n_tok_tiles
    return out.reshape(batch, seq, d)
```

```python
import functools
import math

import jax
import jax.numpy as jnp
import numpy as np
from jax import lax
from jax.experimental import pallas as pl
from jax.experimental.pallas import tpu as pltpu

F32 = jnp.float32
BF16 = jnp.bfloat16

D_MODEL = 1024
CHUNK = 64
N_META = 16
ROPE_THETA = 500000.0
EPS = 1e-6
LANES = 128
HEAD = 64
WIDTH = 512
MLA_HEADS = 8
MLA_ROPE = 32
MLA_Q_RANK = 384
MLA_KV_RANK = 256
DIFF_ROT = 16

META_ROWS = 256
ROW_TILE = 256
ATT_TILE = 256
ROW_CHUNK = 128
ATT_GROUPS = 4
NEG_BIG = -1e30
LOG2E = 1.4426950408889634
SB_DEAD_RUN = -104.0
VMEM_LIMIT = 56 * 1024 * 1024

C_GATE = 0
C_SBQ = 3072
C_SBK = 3584
C_SBV = 4096
C_SBZ = 4608
C_DQ = 5120
C_DK = 5632
C_DV = 6144
C_DZ = 6656
C_MZ = 7168
C_LAT = 7680
LAT_W = 768
N_PROJ = 8448
PROJ_TN = 768


def _params(sem):
    return pltpu.CompilerParams(dimension_semantics=sem, vmem_limit_bytes=VMEM_LIMIT)


def _rms(x32, g):
    ms = jnp.mean(x32 * x32, axis=-1, keepdims=True)
    return x32 * lax.rsqrt(ms + EPS) * g


def _residual_rows(tok_ref, meta_ref, n_tok_tiles):
    return jnp.where(pl.program_id(0) < n_tok_tiles, tok_ref[...], meta_ref[...])


def _residual_specs(tm, n_tok_tiles, meta_block):
    return [pl.BlockSpec((tm, D_MODEL), lambda i: (jnp.minimum(i, n_tok_tiles - 1), 0)),
            pl.BlockSpec((tm, D_MODEL), lambda i: (meta_block, 0))]


def _norm_kernel(tok_ref, meta_ref, g_ref, o_ref, *, n_tok_tiles):
    o_ref[...] = _rms(_residual_rows(tok_ref, meta_ref, n_tok_tiles), g_ref[...]).astype(o_ref.dtype)


def _norm_call(h_tok, h_meta, g, tm):
    n_tok_tiles = h_tok.shape[0] // tm
    rows = h_tok.shape[0] + h_meta.shape[0]
    return pl.pallas_call(
        functools.partial(_norm_kernel, n_tok_tiles=n_tok_tiles),
        grid=(rows // tm,),
        in_specs=_residual_specs(tm, n_tok_tiles, 0) + [pl.BlockSpec((1, D_MODEL), lambda i: (0, 0))],
        out_specs=pl.BlockSpec((tm, D_MODEL), lambda i: (i, 0)),
        out_shape=jax.ShapeDtypeStruct((rows, D_MODEL), BF16),
        compiler_params=_params(("parallel",)),
        name="norm_in",
    )(h_tok, h_meta, g)


def _proj_kernel(a_ref, w_ref, o_ref):
    o_ref[...] = jnp.dot(a_ref[...], w_ref[...], preferred_element_type=F32).astype(o_ref.dtype)


def _proj_call(hn, w, tm):
    rows = hn.shape[0]
    return pl.pallas_call(
        _proj_kernel,
        grid=(rows // tm, N_PROJ // PROJ_TN),
        in_specs=[pl.BlockSpec((tm, D_MODEL), lambda i, j: (i, 0)),
                  pl.BlockSpec((D_MODEL, PROJ_TN), lambda i, j: (0, j))],
        out_specs=pl.BlockSpec((tm, PROJ_TN), lambda i, j: (i, j)),
        out_shape=jax.ShapeDtypeStruct((rows, N_PROJ), BF16),
        compiler_params=_params(("parallel", "arbitrary")),
        name="in_proj",
    )(hn, w)


def _rot(x, c, s1, s2, shift):
    return x * c + pltpu.roll(x, LANES - shift, 1) * s1 + pltpu.roll(x, shift, 1) * s2


def _prep_kernel(lat_ref, dq_ref, dk_ref, gq_ref, gkv_ref, wuq_ref, wuk_ref, wuv_ref,
                 cm_ref, s1m_ref, s2m_ref, cd_ref, s1d_ref, s2d_ref,
                 qm_ref, km_ref, vm_ref, dqo_ref, dko_ref):
    lat = lat_ref[...].astype(F32)
    ckv = lat[:, :MLA_KV_RANK]
    cq = lat[:, MLA_KV_RANK:MLA_KV_RANK + MLA_Q_RANK]
    kr = lat[:, MLA_KV_RANK + MLA_Q_RANK:]
    ncq = _rms(cq, gq_ref[...]).astype(BF16)
    nckv = _rms(ckv, gkv_ref[...]).astype(BF16)
    cm, s1m, s2m = cm_ref[...], s1m_ref[...], s2m_ref[...]
    scale_b = LOG2E / math.sqrt(HEAD + MLA_ROPE)
    k_rope = _rot(kr, cm, s1m, s2m, MLA_ROPE // 2)
    vm_ref[...] = jnp.dot(nckv, wuv_ref[...], preferred_element_type=F32).astype(BF16)
    for h in range(MLA_HEADS):
        sl = slice(h * LANES, (h + 1) * LANES)
        qf = jnp.dot(ncq, wuq_ref[:, sl], preferred_element_type=F32)
        qm_ref[:, sl] = (_rot(qf, cm, s1m, s2m, MLA_ROPE // 2) * scale_b).astype(BF16)
        kf = jnp.dot(nckv, wuk_ref[:, sl], preferred_element_type=F32)
        km_ref[:, sl] = (kf + k_rope).astype(BF16)
    cd, s1d, s2d = cd_ref[...], s1d_ref[...], s2d_ref[...]
    for h in range(WIDTH // LANES):
        sl = slice(h * LANES, (h + 1) * LANES)
        dqo_ref[:, sl] = (_rot(dq_ref[:, sl].astype(F32), cd, s1d, s2d, DIFF_ROT // 2) * (LOG2E / math.sqrt(HEAD))).astype(BF16)
        dko_ref[:, sl] = _rot(dk_ref[:, sl].astype(F32), cd, s1d, s2d, DIFF_ROT // 2).astype(BF16)


def _prep_call(proj, gq, gkv, wuq, wuk, wuv, tabs, tm, tok_rows, seq):
    rows = proj.shape[0]
    n_tok_tiles = tok_rows // tm
    per_seq = seq // tm

    def tab_map(i):
        return (jnp.where(i < n_tok_tiles, i % per_seq, per_seq), 0)

    row = lambda w, c: pl.BlockSpec((tm, w), lambda i: (i, c))
    full = lambda a: pl.BlockSpec(a.shape, lambda i: (0, 0))
    tab = pl.BlockSpec((tm, LANES), tab_map)
    out = lambda w: jax.ShapeDtypeStruct((rows, w), BF16)
    return pl.pallas_call(
        _prep_kernel,
        grid=(rows // tm,),
        in_specs=[row(LAT_W, C_LAT // LAT_W), row(WIDTH, C_DQ // WIDTH), row(WIDTH, C_DK // WIDTH),
                  full(gq), full(gkv), full(wuq), full(wuk), full(wuv)] + [tab] * 6,
        out_specs=[row(MLA_HEADS * LANES, 0), row(MLA_HEADS * LANES, 0), row(WIDTH, 0),
                   row(WIDTH, 0), row(WIDTH, 0)],
        out_shape=[out(MLA_HEADS * LANES), out(MLA_HEADS * LANES), out(WIDTH), out(WIDTH), out(WIDTH)],
        compiler_params=_params(("parallel",)),
        name="prep",
    )(proj, proj, proj, gq, gkv, wuq, wuk, wuv, *tabs)


def _sigmoid(x):
    return 0.5 * jnp.tanh(0.5 * x) + 0.5


def _merge_kernel(osb_ref, omla_ref, odiff_ref, zsb_ref, zmla_ref, zdiff_ref, g0_ref, g1_ref, g2_ref,
                  htok_ref, hmeta_ref, bg_ref, wsb_ref, wmla_ref, wdiff_ref, wout_ref, gn_ref, *out_refs,
                  last, n_tok_tiles):
    def branch(o_ref, z_ref, w_ref):
        z = z_ref[...].astype(F32)
        a = o_ref[...].astype(F32) * (z * _sigmoid(z))
        return jnp.dot(a.astype(BF16), w_ref[...], preferred_element_type=F32)

    bg = bg_ref[...]
    merged = _sigmoid(g0_ref[...].astype(F32) + bg[:, :D_MODEL]) * branch(osb_ref, zsb_ref, wsb_ref)
    merged += _sigmoid(g1_ref[...].astype(F32) + bg[:, D_MODEL:2 * D_MODEL]) * branch(omla_ref, zmla_ref, wmla_ref)
    merged += _sigmoid(g2_ref[...].astype(F32) + bg[:, 2 * D_MODEL:]) * branch(odiff_ref, zdiff_ref, wdiff_ref)
    h_old = _residual_rows(htok_ref, hmeta_ref, n_tok_tiles)
    h_new = h_old + jnp.dot(merged.astype(BF16), wout_ref[...], preferred_element_type=F32)
    normed = _rms(h_new, gn_ref[...])
    if last:
        out_refs[0][...] = normed
    else:
        out_refs[0][...] = h_new
        out_refs[1][...] = normed.astype(BF16)


def _merge_call(o_sb, o_mla, o_diff, proj, h_tok, h_meta, meta_block, n_tok_tiles, bg, w_sb, w_mla, w_diff, w_out,
                g_next, tm, out_rows, last):
    row = lambda w, c: pl.BlockSpec((tm, w), lambda i: (i, c))
    full = lambda a: pl.BlockSpec(a.shape, lambda i: (0, 0))
    if last:
        out_specs = [row(D_MODEL, 0)]
        out_shape = [jax.ShapeDtypeStruct((out_rows, D_MODEL), F32)]
    else:
        out_specs = [row(D_MODEL, 0), row(D_MODEL, 0)]
        out_shape = [jax.ShapeDtypeStruct((out_rows, D_MODEL), F32),
                     jax.ShapeDtypeStruct((out_rows, D_MODEL), BF16)]
    return pl.pallas_call(
        functools.partial(_merge_kernel, last=last, n_tok_tiles=n_tok_tiles),
        grid=(out_rows // tm,),
        in_specs=[row(WIDTH, 0), row(WIDTH, 0), row(WIDTH, 0),
                  row(WIDTH, C_SBZ // WIDTH), row(WIDTH, C_MZ // WIDTH), row(WIDTH, C_DZ // WIDTH),
                  row(D_MODEL, 0), row(D_MODEL, 1), row(D_MODEL, 2)]
                 + _residual_specs(tm, n_tok_tiles, meta_block)
                 + [full(bg), full(w_sb), full(w_mla), full(w_diff), full(w_out), full(g_next)],
        out_specs=out_specs,
        out_shape=out_shape,
        compiler_params=_params(("parallel",)),
        name="merge_last" if last else "merge",
    )(o_sb, o_mla, o_diff, proj, proj, proj, proj, proj, proj, h_tok, h_meta, bg, w_sb, w_mla, w_diff, w_out, g_next)


def _qk(q, k):
    return lax.dot_general(q, k, (((1,), (1,)), ((), ())), preferred_element_type=F32)


def _iota2(shape, axis):
    return lax.broadcasted_iota(jnp.int32, shape, axis)


def _half_mask(rows, half):
    lane = _iota2((rows, LANES), 1)
    return (lane >= HEAD) if half else (lane < HEAD)


def _pv(p, v):
    return jnp.dot(p.astype(BF16), v, preferred_element_type=F32)


def _incl_tri(n):
    return (_iota2((n, n), 0) >= _iota2((n, n), 1)).astype(BF16)


def _sb_logw(z, tri):
    nz = -z
    lk = jnp.minimum(nz, 0.0) - jnp.log(1.0 + jnp.exp(jnp.minimum(z, nz)))
    hi = lk.astype(BF16)
    lo = (lk - hi.astype(F32)).astype(BF16)
    if tri.shape[0] == 2 * z.shape[1]:
        cum = jnp.dot(jnp.concatenate([hi, lo], axis=1), tri, preferred_element_type=F32)
    else:
        cum = jnp.dot(hi, tri, preferred_element_type=F32) + jnp.dot(lo, tri, preferred_element_type=F32)
    return z + cum, cum[:, 0:1]


def _wide(x, width):
    return jnp.concatenate([x] * (width // LANES), axis=-1)


def _row_blocks(t):
    return [pl.ds(r * ROW_CHUNK, ROW_CHUNK) for r in range(t // ROW_CHUNK)]


def _stream_cols(n_streams):
    return [slice((n // 2) * LANES, (n // 2 + 1) * LANES) for n in range(n_streams)]


def _half_queries(q_ref, t, scale):
    qs = []
    for n, c in enumerate(_stream_cols(2 * (q_ref.shape[1] // LANES))):
        qg = q_ref[:, c] if scale is None else q_ref[:, c] * scale
        qs.append(jnp.where(_half_mask(t, n % 2), qg, jnp.zeros_like(qg)))
    return qs


def _join_halves(o_ref, outs, t):
    for g in range(len(outs) // 2):
        o_ref[:, g * LANES:(g + 1) * LANES] = jnp.where(_half_mask(t, 0), outs[2 * g], outs[2 * g + 1]).astype(o_ref.dtype)


def _sb_kernel(q_ref, k_ref, v_ref, km_ref, vm_ref, o_ref, acc_scr, run_scr, *, n_tok_tiles):
    i = pl.program_id(2)
    t = ATT_TILE
    qs = _half_queries(q_ref, t, 0.125)
    n_s = len(qs)
    cols = _stream_cols(n_s)
    tri = jnp.concatenate([_incl_tri(t)] * 2, axis=0)
    tri_m = _incl_tri(N_META)
    kms = [km_ref[:, c] for c in cols]
    vms = [vm_ref[:, c] for c in cols]
    blocks = _row_blocks(t)

    def key_step(kbs, vbs, tri_k, masks):
        def scores(n):
            return _qk(qs[n], kbs[n])

        def log_weights(n, z):
            parts = []
            for b in range(len(blocks)):
                zb = z[b * ROW_CHUNK:(b + 1) * ROW_CHUNK]
                if masks is not None:
                    zb = jnp.where(masks[b], zb, NEG_BIG)
                parts.append(_sb_logw(zb, tri_k))
            return parts

        def fold(n, parts):
            runs = []
            for r, (part, total) in zip(blocks, parts):
                run = run_scr[n, r, :]
                w = jnp.exp(part + (_wide(run, part.shape[1]) if part.shape[1] > LANES else run[:, :part.shape[1]]))
                acc_scr[n, r, :] += _pv(w, vbs[n])
                run_scr[n, r, :] = run + total
                runs.append(run + total)
            return runs

        z = {0: scores(0)}
        if n_s > 1:
            z[1] = scores(1)
        lw = {0: log_weights(0, z.pop(0))}
        runs = []
        for n in range(n_s):
            if n + 2 < n_s:
                z[n + 2] = scores(n + 2)
            if n + 1 < n_s:
                lw[n + 1] = log_weights(n + 1, z.pop(n + 1))
            runs += fold(n, lw.pop(n))
        return jnp.max(functools.reduce(jnp.maximum, runs))

    def tiles_at(ref, p):
        rows = pl.ds(pl.multiple_of((i - p) * t, t), t)
        per_group = [ref[rows, cols[2 * g]] for g in range(n_s // 2)]
        return [per_group[n // 2] for n in range(n_s)]

    def tok():
        causal = [_iota2((ROW_CHUNK, t), 1) < _iota2((ROW_CHUNK, t), 0) + b * ROW_CHUNK for b in range(len(blocks))]
        acc_scr[...] = jnp.zeros_like(acc_scr)
        run_scr[...] = jnp.zeros_like(run_scr)
        first = key_step(tiles_at(k_ref, 0), tiles_at(v_ref, 0), tri, causal)

        def alive(c):
            return jnp.logical_and(c[0] <= i, c[1] >= SB_DEAD_RUN)

        def older(c):
            p = c[0]
            return p + 1, key_step(tiles_at(k_ref, p), tiles_at(v_ref, p), tri, None)

        _, largest = lax.while_loop(alive, older, (jnp.int32(1), first))

        @pl.when(largest >= SB_DEAD_RUN)
        def _():
            key_step(kms, vms, tri_m, None)

        return [acc_scr[n] for n in range(n_s)]

    def meta():
        causal_m = _iota2((t, N_META), 1) < _iota2((t, N_META), 0)
        zm = [_qk(qh, km) for qh, km in zip(qs, kms)]
        return [_pv(jnp.exp(_sb_logw(jnp.where(causal_m, z, NEG_BIG), tri_m)[0]), vm) for z, vm in zip(zm, vms)]

    _join_halves(o_ref, lax.cond(i < n_tok_tiles, tok, meta), t)


def _sb_scratch(n_s):
    t = ATT_TILE
    return [pltpu.VMEM((n_s, t, LANES), F32), pltpu.VMEM((n_s, t, LANES), F32)]


def _row_max(s):
    return jnp.max(s, axis=-1, keepdims=True)


def _row_sum(p):
    return jnp.sum(p, axis=-1, keepdims=True)


def _softmax_attend(i, n_tok_tiles, qs, k_tile, v_tile, kms, vms, acc_scr, m_scr, l_scr):
    t = ATT_TILE
    n_s = len(qs)
    blocks = _row_blocks(t)

    def pipelined(issue, fold):
        pending = issue(0)
        for n in range(n_s):
            current, pending = pending, (issue(n + 1) if n + 1 < n_s else None)
            fold(n, current)

    def sweep(start, width):
        def fold(n, s):
            vb = v_tile(n, start, width)
            for b, r in enumerate(blocks):
                sb = s[b * ROW_CHUNK:(b + 1) * ROW_CHUNK]
                m_old = m_scr[n, r, :]
                m_new = jnp.maximum(m_old, _row_max(sb))
                alpha = jnp.exp2(m_old - m_new)
                p = jnp.exp2(sb - _wide(m_new, width))
                m_scr[n, r, :] = m_new
                l_scr[n, r, :] = alpha * l_scr[n, r, :] + _row_sum(p)
                acc_scr[n, r, :] = alpha * acc_scr[n, r, :] + _pv(p, vb)

        pipelined(lambda n: _qk(qs[n], k_tile(n, start, width)), fold)

    def tok():
        diag = pl.multiple_of(i * t, t)
        chunk_ok = [_iota2((ROW_CHUNK, t), 1) // CHUNK <= _iota2((ROW_CHUNK, t), 0) // CHUNK + b * (ROW_CHUNK // CHUNK)
                    for b in range(len(blocks))]

        def first(n, scores):
            sd_all, sm_all = scores
            vd = v_tile(n, diag, t)
            for b, r in enumerate(blocks):
                rows = slice(b * ROW_CHUNK, (b + 1) * ROW_CHUNK)
                sd = jnp.where(chunk_ok[b], sd_all[rows], NEG_BIG)
                sm = sm_all[rows]
                m = jnp.maximum(_row_max(sd), _row_max(sm))
                pd, pm = jnp.exp2(sd - m), jnp.exp2(sm - m)
                m_scr[n, r, :] = jnp.broadcast_to(m, (ROW_CHUNK, LANES))
                l_scr[n, r, :] = jnp.broadcast_to(_row_sum(pd) + _row_sum(pm), (ROW_CHUNK, LANES))
                acc_scr[n, r, :] = _pv(pd, vd) + _pv(pm, vms[n])

        pipelined(lambda n: (_qk(qs[n], k_tile(n, diag, t)), _qk(qs[n], kms[n])), first)

        def wide_tile(j, carry):
            sweep(pl.multiple_of(j * 2 * t, 2 * t), 2 * t)
            return carry

        lax.fori_loop(0, i // 2, wide_tile, 0)

        @pl.when(i % 2 == 1)
        def _():
            sweep(pl.multiple_of((i - 1) * t, t), t)

        return [acc_scr[n] * (1.0 / l_scr[n]) for n in range(n_s)]

    def meta():
        sms = [_qk(q, km) for q, km in zip(qs, kms)]
        pms = [jnp.exp2(sm - _row_max(sm)) for sm in sms]
        return [_pv(pm, vm) * (1.0 / _row_sum(pm)) for pm, vm in zip(pms, vms)]

    return lax.cond(i < n_tok_tiles, tok, meta)


def _softmax_scratch(n_s):
    t = ATT_TILE
    return [pltpu.VMEM((n_s, t, LANES), F32), pltpu.VMEM((n_s, t, LANES), F32), pltpu.VMEM((n_s, t, LANES), F32)]


def _mla_kernel(q_ref, k_ref, v_ref, km_ref, vm_ref, o_ref, *scratch, n_tok_tiles):
    i = pl.program_id(2)
    n_s = q_ref.shape[1] // LANES
    qcols = [slice(n * LANES, (n + 1) * LANES) for n in range(n_s)]
    vcols = _stream_cols(n_s)
    outs = _softmax_attend(
        i, n_tok_tiles, [q_ref[:, c] for c in qcols],
        lambda n, st, w: k_ref[pl.ds(st, w), qcols[n]], lambda n, st, w: v_ref[pl.ds(st, w), vcols[n]],
        [km_ref[:, c] for c in qcols], [vm_ref[:, c] for c in vcols], *scratch)
    _join_halves(o_ref, outs, ATT_TILE)


def _diff_kernel(q_ref, k_ref, v_ref, km_ref, vm_ref, lam_ref, g_ref, o_ref, *scratch, n_tok_tiles, lam_init):
    i = pl.program_id(2)
    qs = _half_queries(q_ref, ATT_TILE, None)
    cols = _stream_cols(len(qs))
    outs = _softmax_attend(
        i, n_tok_tiles, qs,
        lambda n, st, w: k_ref[pl.ds(st, w), cols[n]], lambda n, st, w: v_ref[pl.ds(st, w), cols[n]],
        [km_ref[:, c] for c in cols], [vm_ref[:, c] for c in cols], *scratch)
    lam4 = lam_ref[...]
    lam = (jnp.exp(jnp.sum(lam4[0:1] * lam4[1:2], axis=-1, keepdims=True))
           - jnp.exp(jnp.sum(lam4[2:3] * lam4[3:4], axis=-1, keepdims=True)) + lam_init)
    for g in range(len(qs) // 2):
        o = outs[2 * g] - lam * outs[2 * g + 1]
        o_ref[:, g * LANES:(g + 1) * LANES] = (_rms(o, g_ref[...]) * (1.0 - lam_init)).astype(o_ref.dtype)


def _attn_call(kernel, name, q, qw, qc, k, kw, kc, v, vc, extra, scratch, batch, seq, groups):
    rows = q.shape[0]
    t = ATT_TILE
    n_tok_tiles = seq // t
    tok_rows = batch * seq
    meta_q = tok_rows // t
    meta_k = tok_rows // N_META
    vw = WIDTH // groups

    def q_map(b, g, i):
        return (jnp.where(i < n_tok_tiles, b * n_tok_tiles + i, meta_q), qc + g)

    def o_map(b, g, i):
        return (jnp.where(i < n_tok_tiles, b * n_tok_tiles + i, meta_q + b), g)

    in_specs = [
        pl.BlockSpec((t, qw), q_map),
        pl.BlockSpec((seq, kw), lambda b, g, i: (b, kc + g)),
        pl.BlockSpec((seq, vw), lambda b, g, i: (b, vc + g)),
        pl.BlockSpec((N_META, kw), lambda b, g, i: (meta_k, kc + g)),
        pl.BlockSpec((N_META, vw), lambda b, g, i: (meta_k, vc + g)),
    ] + [pl.BlockSpec(a.shape, lambda b, g, i: (0, 0)) for a in extra]
    return pl.pallas_call(
        functools.partial(kernel, n_tok_tiles=n_tok_tiles),
        grid=(batch, groups, n_tok_tiles + 1),
        in_specs=in_specs,
        out_specs=pl.BlockSpec((t, vw), o_map),
        out_shape=jax.ShapeDtypeStruct((rows + (batch - 1) * t, WIDTH), BF16),
        scratch_shapes=scratch,
        compiler_params=_params(("parallel", "parallel", "arbitrary")),
        name=name,
    )(q, k, v, k, v, *extra)


def _perm_w_in(w_in):
    sizes = [512, 512, 512, 512, 384, 256, 32, 512, 512, 512, 512, 512, 3072]
    offs = [0]
    for s in sizes:
        offs.append(offs[-1] + s)
    seg = [w_in[..., offs[n]:offs[n + 1]] for n in range(len(sizes))]
    sb_q, sb_k, sb_v, sb_z, cq, ckv, kr, mz, dq, dk, dv, dz, gate = seg
    pad = lambda n: jnp.zeros(w_in.shape[:-1] + (n,), w_in.dtype)
    cols = [gate, sb_q, sb_k, sb_v, sb_z, dq, dk, dv, dz, mz, ckv, cq, pad(HEAD), kr, pad(LANES - HEAD - MLA_ROPE)]
    return jnp.concatenate(cols, axis=-1).astype(BF16)


def _rope_tables(seq):
    pos = np.concatenate([np.arange(seq) + N_META, np.arange(META_ROWS)]).astype(np.float64)[:, None]

    def tables(dim, starts):
        inv = ROPE_THETA ** (-np.arange(0, dim, 2, dtype=np.float64) / dim)
        ang = pos * inv[None, :]
        cos, sin = np.cos(ang), np.sin(ang)
        half = dim // 2
        c = np.ones((pos.shape[0], LANES))
        s1 = np.zeros((pos.shape[0], LANES))
        s2 = np.zeros((pos.shape[0], LANES))
        for st in starts:
            c[:, st:st + half] = cos
            c[:, st + half:st + dim] = cos
            s1[:, st:st + half] = -sin
            s2[:, st + half:st + dim] = sin
        return [jnp.asarray(a, F32) for a in (c, s1, s2)]

    return tables(MLA_ROPE, [HEAD]) + tables(DIFF_ROT, [0, HEAD])


def _pick(rows, candidates):
    for c in candidates:
        if rows % c == 0:
            return c
    raise ValueError(f"no row tile for {rows}")


def kernel(x, meta_tokens, norm_g, w_in, b_gate, mla_cq_g, mla_ckv_g, mla_w_uq, mla_w_ukv, diff_lambda,
           diff_norm_g, w_o_sb, w_o_mla, w_o_diff, w_out, final_g):
    batch, seq, d = x.shape
    depth = norm_g.shape[0]
    assert d == D_MODEL and seq % ATT_TILE == 0 and meta_tokens.shape == (N_META, D_MODEL)
    tok_rows = batch * seq
    rows = tok_rows + META_ROWS
    tm = ROW_TILE
    tm_proj = _pick(rows, (1280, 1024, 768, 512, 256))

    n_tok_tiles = tok_rows // tm
    h_tok = x.reshape(tok_rows, d)
    h_meta = jnp.concatenate([meta_tokens.astype(x.dtype), jnp.zeros((META_ROWS - N_META, d), x.dtype)], axis=0)
    meta_block = 0
    wuq = jnp.pad(mla_w_uq.reshape(depth, MLA_Q_RANK, MLA_HEADS, HEAD + MLA_ROPE),
                  ((0, 0), (0, 0), (0, 0), (0, LANES - HEAD - MLA_ROPE))).reshape(depth, MLA_Q_RANK, -1).astype(BF16)
    wukv = mla_w_ukv.reshape(depth, MLA_KV_RANK, MLA_HEADS, 2 * HEAD)
    wuk = jnp.pad(wukv[..., :HEAD], ((0, 0), (0, 0), (0, 0), (0, LANES - HEAD))).reshape(depth, MLA_KV_RANK, -1).astype(BF16)
    wuv = wukv[..., HEAD:].reshape(depth, MLA_KV_RANK, -1).astype(BF16)
    tabs = _rope_tables(seq)
    row2 = lambda a: a.reshape(1, -1)

    hn = _norm_call(h_tok, h_meta, row2(norm_g[0]), tm)
    out = None
    for l in range(depth):
        last = l == depth - 1
        proj = _proj_call(hn, _perm_w_in(w_in[l]), tm_proj)
        q_m, k_m, v_m, q_d, k_d = _prep_call(proj, row2(mla_cq_g[l]), row2(mla_ckv_g[l]), wuq[l], wuk[l], wuv[l],
                                             tabs, tm, tok_rows, seq)
        gw = ATT_GROUPS * LANES
        n_s = 2 * ATT_GROUPS
        o_sb = _attn_call(_sb_kernel, "sb_attn", proj, gw, C_SBQ // gw, proj, gw, C_SBK // gw,
                          proj, C_SBV // gw, [], _sb_scratch(n_s), batch, seq, WIDTH // gw)
        o_mla = _attn_call(_mla_kernel, "mla_attn", q_m, 2 * gw, 0, k_m, 2 * gw, 0, v_m, 0, [],
                           _softmax_scratch(n_s), batch, seq, WIDTH // gw)
        lam_init = 0.8 - 0.6 * math.exp(-0.3 * l)
        o_diff = _attn_call(functools.partial(_diff_kernel, lam_init=lam_init), "diff_attn",
                            q_d, gw, 0, k_d, gw, 0, proj, C_DV // gw,
                            [diff_lambda[l].astype(F32), row2(diff_norm_g[l])], _softmax_scratch(n_s), batch, seq,
                            WIDTH // gw)
        g_next = row2(final_g if last else norm_g[l + 1])
        res = _merge_call(o_sb, o_mla, o_diff, proj, h_tok, h_meta, meta_block, n_tok_tiles, row2(b_gate[l]),
                          w_o_sb[l].astype(BF16), w_o_mla[l].astype(BF16), w_o_diff[l].astype(BF16),
                          w_out[l].astype(BF16), g_next, tm, tok_rows if last else rows, last)
        if last:
            out = res[0]
        else:
            h, hn = res
            h_tok, h_meta, meta_block = h, h, n_tok_tiles
    return out.reshape(batch, seq, d)
```

```python
import functools
import math

import jax
import jax.numpy as jnp
import numpy as np
from jax import lax
from jax.experimental import pallas as pl
from jax.experimental.pallas import tpu as pltpu

F32 = jnp.float32
BF16 = jnp.bfloat16

D_MODEL = 1024
CHUNK = 64
N_META = 16
ROPE_THETA = 500000.0
EPS = 1e-6
LANES = 128
HEAD = 64
WIDTH = 512
MLA_HEADS = 8
MLA_ROPE = 32
MLA_Q_RANK = 384
MLA_KV_RANK = 256
DIFF_ROT = 16

META_ROWS = 256
ROW_TILE = 256
ATT_TILE = 256
ROW_CHUNK = 128
SCORE_LOOKAHEAD = 8
ATT_GROUPS = 4
NEG_BIG = -1e30
LOG2E = 1.4426950408889634
SB_DEAD_RUN = -104.0
VMEM_LIMIT = 56 * 1024 * 1024

C_GATE = 0
C_SBQ = 3072
C_SBK = 3584
C_SBV = 4096
C_SBZ = 4608
C_DQ = 5120
C_DK = 5632
C_DV = 6144
C_DZ = 6656
C_MZ = 7168
C_LAT = 7680
LAT_W = 768
N_PROJ = 8448
PROJ_TN = 768


def _params(sem):
    return pltpu.CompilerParams(dimension_semantics=sem, vmem_limit_bytes=VMEM_LIMIT)


def _rms(x32, g):
    ms = jnp.mean(x32 * x32, axis=-1, keepdims=True)
    return x32 * lax.rsqrt(ms + EPS) * g


def _residual_rows(tok_ref, meta_ref, n_tok_tiles):
    return jnp.where(pl.program_id(0) < n_tok_tiles, tok_ref[...], meta_ref[...])


def _residual_specs(tm, n_tok_tiles, meta_block):
    return [pl.BlockSpec((tm, D_MODEL), lambda i: (jnp.minimum(i, n_tok_tiles - 1), 0)),
            pl.BlockSpec((tm, D_MODEL), lambda i: (meta_block, 0))]


def _norm_kernel(tok_ref, meta_ref, g_ref, o_ref, *, n_tok_tiles):
    o_ref[...] = _rms(_residual_rows(tok_ref, meta_ref, n_tok_tiles), g_ref[...]).astype(o_ref.dtype)


def _norm_call(h_tok, h_meta, g, tm):
    n_tok_tiles = h_tok.shape[0] // tm
    rows = h_tok.shape[0] + h_meta.shape[0]
    return pl.pallas_call(
        functools.partial(_norm_kernel, n_tok_tiles=n_tok_tiles),
        grid=(rows // tm,),
        in_specs=_residual_specs(tm, n_tok_tiles, 0) + [pl.BlockSpec((1, D_MODEL), lambda i: (0, 0))],
        out_specs=pl.BlockSpec((tm, D_MODEL), lambda i: (i, 0)),
        out_shape=jax.ShapeDtypeStruct((rows, D_MODEL), BF16),
        compiler_params=_params(("parallel",)),
        name="norm_in",
    )(h_tok, h_meta, g)


def _proj_kernel(a_ref, w_ref, o_ref):
    o_ref[...] = jnp.dot(a_ref[...], w_ref[...], preferred_element_type=F32).astype(o_ref.dtype)


def _proj_call(hn, w, tm):
    rows = hn.shape[0]
    return pl.pallas_call(
        _proj_kernel,
        grid=(rows // tm, N_PROJ // PROJ_TN),
        in_specs=[pl.BlockSpec((tm, D_MODEL), lambda i, j: (i, 0)),
                  pl.BlockSpec((D_MODEL, PROJ_TN), lambda i, j: (0, j))],
        out_specs=pl.BlockSpec((tm, PROJ_TN), lambda i, j: (i, j)),
        out_shape=jax.ShapeDtypeStruct((rows, N_PROJ), BF16),
        compiler_params=_params(("parallel", "arbitrary")),
        name="in_proj",
    )(hn, w)


def _rot(x, c, s1, s2, shift):
    return x * c + pltpu.roll(x, LANES - shift, 1) * s1 + pltpu.roll(x, shift, 1) * s2


def _prep_kernel(lat_ref, dq_ref, dk_ref, gq_ref, gkv_ref, wuq_ref, wuk_ref, wuv_ref,
                 cm_ref, s1m_ref, s2m_ref, cd_ref, s1d_ref, s2d_ref,
                 qm_ref, km_ref, vm_ref, dqo_ref, dko_ref):
    lat = lat_ref[...].astype(F32)
    ckv = lat[:, :MLA_KV_RANK]
    cq = lat[:, MLA_KV_RANK:MLA_KV_RANK + MLA_Q_RANK]
    kr = lat[:, MLA_KV_RANK + MLA_Q_RANK:]
    ncq = _rms(cq, gq_ref[...]).astype(BF16)
    nckv = _rms(ckv, gkv_ref[...]).astype(BF16)
    cm, s1m, s2m = cm_ref[...], s1m_ref[...], s2m_ref[...]
    scale_b = LOG2E / math.sqrt(HEAD + MLA_ROPE)
    k_rope = _rot(kr, cm, s1m, s2m, MLA_ROPE // 2)
    vm_ref[...] = jnp.dot(nckv, wuv_ref[...], preferred_element_type=F32).astype(BF16)
    for h in range(MLA_HEADS):
        sl = slice(h * LANES, (h + 1) * LANES)
        qf = jnp.dot(ncq, wuq_ref[:, sl], preferred_element_type=F32)
        qm_ref[:, sl] = (_rot(qf, cm, s1m, s2m, MLA_ROPE // 2) * scale_b).astype(BF16)
        kf = jnp.dot(nckv, wuk_ref[:, sl], preferred_element_type=F32)
        km_ref[:, sl] = (kf + k_rope).astype(BF16)
    cd, s1d, s2d = cd_ref[...], s1d_ref[...], s2d_ref[...]
    for h in range(WIDTH // LANES):
        sl = slice(h * LANES, (h + 1) * LANES)
        dqo_ref[:, sl] = (_rot(dq_ref[:, sl].astype(F32), cd, s1d, s2d, DIFF_ROT // 2) * (LOG2E / math.sqrt(HEAD))).astype(BF16)
        dko_ref[:, sl] = _rot(dk_ref[:, sl].astype(F32), cd, s1d, s2d, DIFF_ROT // 2).astype(BF16)


def _prep_call(proj, gq, gkv, wuq, wuk, wuv, tabs, tm, tok_rows, seq):
    rows = proj.shape[0]
    n_tok_tiles = tok_rows // tm
    per_seq = seq // tm

    def tab_map(i):
        return (jnp.where(i < n_tok_tiles, i % per_seq, per_seq), 0)

    row = lambda w, c: pl.BlockSpec((tm, w), lambda i: (i, c))
    full = lambda a: pl.BlockSpec(a.shape, lambda i: (0, 0))
    tab = pl.BlockSpec((tm, LANES), tab_map)
    out = lambda w: jax.ShapeDtypeStruct((rows, w), BF16)
    return pl.pallas_call(
        _prep_kernel,
        grid=(rows // tm,),
        in_specs=[row(LAT_W, C_LAT // LAT_W), row(WIDTH, C_DQ // WIDTH), row(WIDTH, C_DK // WIDTH),
                  full(gq), full(gkv), full(wuq), full(wuk), full(wuv)] + [tab] * 6,
        out_specs=[row(MLA_HEADS * LANES, 0), row(MLA_HEADS * LANES, 0), row(WIDTH, 0),
                   row(WIDTH, 0), row(WIDTH, 0)],
        out_shape=[out(MLA_HEADS * LANES), out(MLA_HEADS * LANES), out(WIDTH), out(WIDTH), out(WIDTH)],
        compiler_params=_params(("parallel",)),
        name="prep",
    )(proj, proj, proj, gq, gkv, wuq, wuk, wuv, *tabs)


def _sigmoid(x):
    return 0.5 * jnp.tanh(0.5 * x) + 0.5


def _merge_kernel(osb_ref, omla_ref, odiff_ref, zsb_ref, zmla_ref, zdiff_ref, g0_ref, g1_ref, g2_ref,
                  htok_ref, hmeta_ref, bg_ref, wsb_ref, wmla_ref, wdiff_ref, wout_ref, gn_ref, *out_refs,
                  last, n_tok_tiles):
    def branch(o_ref, z_ref, w_ref):
        z = z_ref[...].astype(F32)
        a = o_ref[...].astype(F32) * (z * _sigmoid(z))
        return jnp.dot(a.astype(BF16), w_ref[...], preferred_element_type=F32)

    bg = bg_ref[...]
    merged = _sigmoid(g0_ref[...].astype(F32) + bg[:, :D_MODEL]) * branch(osb_ref, zsb_ref, wsb_ref)
    merged += _sigmoid(g1_ref[...].astype(F32) + bg[:, D_MODEL:2 * D_MODEL]) * branch(omla_ref, zmla_ref, wmla_ref)
    merged += _sigmoid(g2_ref[...].astype(F32) + bg[:, 2 * D_MODEL:]) * branch(odiff_ref, zdiff_ref, wdiff_ref)
    h_old = _residual_rows(htok_ref, hmeta_ref, n_tok_tiles)
    h_new = h_old + jnp.dot(merged.astype(BF16), wout_ref[...], preferred_element_type=F32)
    normed = _rms(h_new, gn_ref[...])
    if last:
        out_refs[0][...] = normed
    else:
        out_refs[0][...] = h_new
        out_refs[1][...] = normed.astype(BF16)


def _merge_call(o_sb, o_mla, o_diff, proj, h_tok, h_meta, meta_block, n_tok_tiles, bg, w_sb, w_mla, w_diff, w_out,
                g_next, tm, out_rows, last):
    row = lambda w, c: pl.BlockSpec((tm, w), lambda i: (i, c))
    full = lambda a: pl.BlockSpec(a.shape, lambda i: (0, 0))
    if last:
        out_specs = [row(D_MODEL, 0)]
        out_shape = [jax.ShapeDtypeStruct((out_rows, D_MODEL), F32)]
    else:
        out_specs = [row(D_MODEL, 0), row(D_MODEL, 0)]
        out_shape = [jax.ShapeDtypeStruct((out_rows, D_MODEL), F32),
                     jax.ShapeDtypeStruct((out_rows, D_MODEL), BF16)]
    return pl.pallas_call(
        functools.partial(_merge_kernel, last=last, n_tok_tiles=n_tok_tiles),
        grid=(out_rows // tm,),
        in_specs=[row(WIDTH, 0), row(WIDTH, 0), row(WIDTH, 0),
                  row(WIDTH, C_SBZ // WIDTH), row(WIDTH, C_MZ // WIDTH), row(WIDTH, C_DZ // WIDTH),
                  row(D_MODEL, 0), row(D_MODEL, 1), row(D_MODEL, 2)]
                 + _residual_specs(tm, n_tok_tiles, meta_block)
                 + [full(bg), full(w_sb), full(w_mla), full(w_diff), full(w_out), full(g_next)],
        out_specs=out_specs,
        out_shape=out_shape,
        compiler_params=_params(("parallel",)),
        name="merge_last" if last else "merge",
    )(o_sb, o_mla, o_diff, proj, proj, proj, proj, proj, proj, h_tok, h_meta, bg, w_sb, w_mla, w_diff, w_out, g_next)


def _qk(q, k):
    return lax.dot_general(q, k, (((1,), (1,)), ((), ())), preferred_element_type=F32)


def _iota2(shape, axis):
    return lax.broadcasted_iota(jnp.int32, shape, axis)


def _half_mask(rows, half):
    lane = _iota2((rows, LANES), 1)
    return (lane >= HEAD) if half else (lane < HEAD)


def _pv(p, v):
    return jnp.dot(p.astype(BF16), v, preferred_element_type=F32)


def _incl_tri(n):
    return (_iota2((n, n), 0) >= _iota2((n, n), 1)).astype(BF16)


def _sb_logw(z, tri):
    nz = -z
    lk = jnp.minimum(nz, 0.0) - jnp.log(1.0 + jnp.exp(jnp.minimum(z, nz)))
    hi = lk.astype(BF16)
    lo = (lk - hi.astype(F32)).astype(BF16)
    if tri.shape[0] == 2 * z.shape[1]:
        cum = jnp.dot(jnp.concatenate([hi, lo], axis=1), tri, preferred_element_type=F32)
    else:
        cum = jnp.dot(hi, tri, preferred_element_type=F32) + jnp.dot(lo, tri, preferred_element_type=F32)
    return z + cum, cum[:, 0:1]


def _wide(x, width):
    return jnp.concatenate([x] * (width // LANES), axis=-1)


def _row_blocks(t):
    return [pl.ds(r * ROW_CHUNK, ROW_CHUNK) for r in range(t // ROW_CHUNK)]


def _stream_cols(n_streams):
    return [slice((n // 2) * LANES, (n // 2 + 1) * LANES) for n in range(n_streams)]


def _half_queries(q_ref, t, scale):
    qs = []
    for n, c in enumerate(_stream_cols(2 * (q_ref.shape[1] // LANES))):
        qg = q_ref[:, c] if scale is None else q_ref[:, c] * scale
        qs.append(jnp.where(_half_mask(t, n % 2), qg, jnp.zeros_like(qg)))
    return qs


def _join_halves(o_ref, outs, t):
    for g in range(len(outs) // 2):
        o_ref[:, g * LANES:(g + 1) * LANES] = jnp.where(_half_mask(t, 0), outs[2 * g], outs[2 * g + 1]).astype(o_ref.dtype)


def _sb_kernel(q_ref, k_ref, v_ref, km_ref, vm_ref, o_ref, acc_scr, run_scr, *, n_tok_tiles):
    i = pl.program_id(2)
    t = ATT_TILE
    qs = _half_queries(q_ref, t, 0.125)
    n_s = len(qs)
    cols = _stream_cols(n_s)
    tri = jnp.concatenate([_incl_tri(t)] * 2, axis=0)
    tri_m = _incl_tri(N_META)
    kms = [km_ref[:, c] for c in cols]
    vms = [vm_ref[:, c] for c in cols]
    blocks = _row_blocks(t)

    def key_step(kbs, vbs, tri_k, masks):
        def scores(n):
            return _qk(qs[n], kbs[n])

        def log_weights(n, z):
            parts = []
            for b in range(len(blocks)):
                zb = z[b * ROW_CHUNK:(b + 1) * ROW_CHUNK]
                if masks is not None:
                    zb = jnp.where(masks[b], zb, NEG_BIG)
                parts.append(_sb_logw(zb, tri_k))
            return parts

        def fold(n, parts):
            runs = []
            for r, (part, total) in zip(blocks, parts):
                run = run_scr[n, r, :]
                w = jnp.exp(part + (_wide(run, part.shape[1]) if part.shape[1] > LANES else run[:, :part.shape[1]]))
                acc_scr[n, r, :] += _pv(w, vbs[n])
                run_scr[n, r, :] = run + total
                runs.append(run + total)
            return runs

        z = {0: scores(0)}
        if n_s > 1:
            z[1] = scores(1)
        lw = {0: log_weights(0, z.pop(0))}
        runs = []
        for n in range(n_s):
            if n + 2 < n_s:
                z[n + 2] = scores(n + 2)
            if n + 1 < n_s:
                lw[n + 1] = log_weights(n + 1, z.pop(n + 1))
            runs += fold(n, lw.pop(n))
        return jnp.max(functools.reduce(jnp.maximum, runs))

    def tiles_at(ref, p):
        rows = pl.ds(pl.multiple_of((i - p) * t, t), t)
        per_group = [ref[rows, cols[2 * g]] for g in range(n_s // 2)]
        return [per_group[n // 2] for n in range(n_s)]

    def tok():
        causal = [_iota2((ROW_CHUNK, t), 1) < _iota2((ROW_CHUNK, t), 0) + b * ROW_CHUNK for b in range(len(blocks))]
        acc_scr[...] = jnp.zeros_like(acc_scr)
        run_scr[...] = jnp.zeros_like(run_scr)
        first = key_step(tiles_at(k_ref, 0), tiles_at(v_ref, 0), tri, causal)

        def alive(c):
            return jnp.logical_and(c[0] <= i, c[1] >= SB_DEAD_RUN)

        def older(c):
            p = c[0]
            return p + 1, key_step(tiles_at(k_ref, p), tiles_at(v_ref, p), tri, None)

        _, largest = lax.while_loop(alive, older, (jnp.int32(1), first))

        @pl.when(largest >= SB_DEAD_RUN)
        def _():
            key_step(kms, vms, tri_m, None)

        return [acc_scr[n] for n in range(n_s)]

    def meta():
        causal_m = _iota2((t, N_META), 1) < _iota2((t, N_META), 0)
        zm = [_qk(qh, km) for qh, km in zip(qs, kms)]
        return [_pv(jnp.exp(_sb_logw(jnp.where(causal_m, z, NEG_BIG), tri_m)[0]), vm) for z, vm in zip(zm, vms)]

    _join_halves(o_ref, lax.cond(i < n_tok_tiles, tok, meta), t)


def _sb_scratch(n_s):
    t = ATT_TILE
    return [pltpu.VMEM((n_s, t, LANES), F32), pltpu.VMEM((n_s, t, LANES), F32)]


def _row_max(s):
    return jnp.max(s, axis=-1, keepdims=True)


def _row_sum(p):
    return jnp.sum(p, axis=-1, keepdims=True)


def _softmax_attend(i, n_tok_tiles, qs, k_tile, v_tile, kms, vms, acc_scr):
    t = ATT_TILE
    n_s = len(qs)

    def scores_t(keys, q):
        return lax.dot_general(keys, q, (((1,), (1,)), ((), ())), preferred_element_type=F32)

    def pv_t(v, p_t):
        return lax.dot_general(v, p_t.astype(BF16), (((0,), (0,)), ((), ())), preferred_element_type=F32)

    def col_max(s):
        return jnp.max(s, axis=0, keepdims=True)

    def col_sum(p):
        return jnp.sum(p, axis=0, keepdims=True)

    streams = range(n_s)

    def sweep(start, width, stats):
        ss = {n: scores_t(k_tile(n, start, width), qs[n]) for n in range(min(SCORE_LOOKAHEAD, n_s))}
        out, updates = [], []
        for n in streams:
            s = ss.pop(n)
            m_new = jnp.maximum(stats[2 * n], col_max(s))
            alpha = jnp.exp2(stats[2 * n] - m_new)
            p = jnp.exp2(s - m_new)
            out += [m_new, alpha * stats[2 * n + 1] + col_sum(p)]
            updates.append((alpha, pv_t(v_tile(n, start, width), p)))
            if n + SCORE_LOOKAHEAD < n_s:
                ss[n + SCORE_LOOKAHEAD] = scores_t(k_tile(n + SCORE_LOOKAHEAD, start, width), qs[n + SCORE_LOOKAHEAD])
        for n, (alpha, pv) in enumerate(updates):
            acc_scr[n] = alpha * acc_scr[n] + pv
        return tuple(out)

    def tok():
        diag = pl.multiple_of(i * t, t)
        chunk_ok = _iota2((t, t), 0) // CHUNK <= _iota2((t, t), 1) // CHUNK
        sds = [jnp.where(chunk_ok, scores_t(k_tile(n, diag, t), qs[n]), NEG_BIG) for n in streams]
        sms = [scores_t(kms[n], qs[n]) for n in streams]
        ms = [jnp.maximum(col_max(sds[n]), col_max(sms[n])) for n in streams]
        pds = [jnp.exp2(sds[n] - ms[n]) for n in streams]
        pms = [jnp.exp2(sms[n] - ms[n]) for n in streams]
        ls = [col_sum(pds[n]) + col_sum(pms[n]) for n in streams]
        for n in streams:
            acc_scr[n] = pv_t(v_tile(n, diag, t), pds[n]) + pv_t(vms[n], pms[n])
        stats = tuple(x for n in streams for x in (ms[n], ls[n]))

        stats = lax.fori_loop(0, i // 2, lambda j, st: sweep(pl.multiple_of(j * 2 * t, 2 * t), 2 * t, st), stats)
        stats = lax.cond(i % 2 == 1, lambda st: sweep(pl.multiple_of((i - 1) * t, t), t, st), lambda st: st, stats)
        return [(acc_scr[n] * (1.0 / stats[2 * n + 1])).T for n in range(n_s)]

    def meta():
        sms = [_qk(q, km) for q, km in zip(qs, kms)]
        pms = [jnp.exp2(sm - _row_max(sm)) for sm in sms]
        return [_pv(pm, vm) * (1.0 / _row_sum(pm)) for pm, vm in zip(pms, vms)]

    return lax.cond(i < n_tok_tiles, tok, meta)


def _softmax_scratch(n_s):
    t = ATT_TILE
    return [pltpu.VMEM((n_s, LANES, t), F32)]


def _mla_kernel(q_ref, k_ref, v_ref, km_ref, vm_ref, o_ref, *scratch, n_tok_tiles):
    i = pl.program_id(2)
    n_s = q_ref.shape[1] // LANES
    qcols = [slice(n * LANES, (n + 1) * LANES) for n in range(n_s)]
    vcols = _stream_cols(n_s)
    outs = _softmax_attend(
        i, n_tok_tiles, [q_ref[:, c] for c in qcols],
        lambda n, st, w: k_ref[pl.ds(st, w), qcols[n]], lambda n, st, w: v_ref[pl.ds(st, w), vcols[n]],
        [km_ref[:, c] for c in qcols], [vm_ref[:, c] for c in vcols], *scratch)
    _join_halves(o_ref, outs, ATT_TILE)


def _diff_kernel(q_ref, k_ref, v_ref, km_ref, vm_ref, lam_ref, g_ref, o_ref, *scratch, n_tok_tiles, lam_init):
    i = pl.program_id(2)
    qs = _half_queries(q_ref, ATT_TILE, None)
    cols = _stream_cols(len(qs))
    outs = _softmax_attend(
        i, n_tok_tiles, qs,
        lambda n, st, w: k_ref[pl.ds(st, w), cols[n]], lambda n, st, w: v_ref[pl.ds(st, w), cols[n]],
        [km_ref[:, c] for c in cols], [vm_ref[:, c] for c in cols], *scratch)
    lam4 = lam_ref[...]
    lam = (jnp.exp(jnp.sum(lam4[0:1] * lam4[1:2], axis=-1, keepdims=True))
           - jnp.exp(jnp.sum(lam4[2:3] * lam4[3:4], axis=-1, keepdims=True)) + lam_init)
    for g in range(len(qs) // 2):
        o = outs[2 * g] - lam * outs[2 * g + 1]
        o_ref[:, g * LANES:(g + 1) * LANES] = (_rms(o, g_ref[...]) * (1.0 - lam_init)).astype(o_ref.dtype)


def _attn_call(kernel, name, q, qw, qc, k, kw, kc, v, vc, extra, scratch, batch, seq, groups):
    rows = q.shape[0]
    t = ATT_TILE
    n_tok_tiles = seq // t
    tok_rows = batch * seq
    meta_q = tok_rows // t
    meta_k = tok_rows // N_META
    vw = WIDTH // groups

    def q_map(b, g, i):
        return (jnp.where(i < n_tok_tiles, b * n_tok_tiles + i, meta_q), qc + g)

    def o_map(b, g, i):
        return (jnp.where(i < n_tok_tiles, b * n_tok_tiles + i, meta_q + b), g)

    in_specs = [
        pl.BlockSpec((t, qw), q_map),
        pl.BlockSpec((seq, kw), lambda b, g, i: (b, kc + g)),
        pl.BlockSpec((seq, vw), lambda b, g, i: (b, vc + g)),
        pl.BlockSpec((N_META, kw), lambda b, g, i: (meta_k, kc + g)),
        pl.BlockSpec((N_META, vw), lambda b, g, i: (meta_k, vc + g)),
    ] + [pl.BlockSpec(a.shape, lambda b, g, i: (0, 0)) for a in extra]
    return pl.pallas_call(
        functools.partial(kernel, n_tok_tiles=n_tok_tiles),
        grid=(batch, groups, n_tok_tiles + 1),
        in_specs=in_specs,
        out_specs=pl.BlockSpec((t, vw), o_map),
        out_shape=jax.ShapeDtypeStruct((rows + (batch - 1) * t, WIDTH), BF16),
        scratch_shapes=scratch,
        compiler_params=_params(("parallel", "parallel", "arbitrary")),
        name=name,
    )(q, k, v, k, v, *extra)


def _perm_w_in(w_in):
    sizes = [512, 512, 512, 512, 384, 256, 32, 512, 512, 512, 512, 512, 3072]
    offs = [0]
    for s in sizes:
        offs.append(offs[-1] + s)
    seg = [w_in[..., offs[n]:offs[n + 1]] for n in range(len(sizes))]
    sb_q, sb_k, sb_v, sb_z, cq, ckv, kr, mz, dq, dk, dv, dz, gate = seg
    pad = lambda n: jnp.zeros(w_in.shape[:-1] + (n,), w_in.dtype)
    cols = [gate, sb_q, sb_k, sb_v, sb_z, dq, dk, dv, dz, mz, ckv, cq, pad(HEAD), kr, pad(LANES - HEAD - MLA_ROPE)]
    return jnp.concatenate(cols, axis=-1).astype(BF16)


def _rope_tables(seq):
    pos = np.concatenate([np.arange(seq) + N_META, np.arange(META_ROWS)]).astype(np.float64)[:, None]

    def tables(dim, starts):
        inv = ROPE_THETA ** (-np.arange(0, dim, 2, dtype=np.float64) / dim)
        ang = pos * inv[None, :]
        cos, sin = np.cos(ang), np.sin(ang)
        half = dim // 2
        c = np.ones((pos.shape[0], LANES))
        s1 = np.zeros((pos.shape[0], LANES))
        s2 = np.zeros((pos.shape[0], LANES))
        for st in starts:
            c[:, st:st + half] = cos
            c[:, st + half:st + dim] = cos
            s1[:, st:st + half] = -sin
            s2[:, st + half:st + dim] = sin
        return [jnp.asarray(a, F32) for a in (c, s1, s2)]

    return tables(MLA_ROPE, [HEAD]) + tables(DIFF_ROT, [0, HEAD])


def _pick(rows, candidates):
    for c in candidates:
        if rows % c == 0:
            return c
    raise ValueError(f"no row tile for {rows}")


def kernel(x, meta_tokens, norm_g, w_in, b_gate, mla_cq_g, mla_ckv_g, mla_w_uq, mla_w_ukv, diff_lambda,
           diff_norm_g, w_o_sb, w_o_mla, w_o_diff, w_out, final_g):
    batch, seq, d = x.shape
    depth = norm_g.shape[0]
    assert d == D_MODEL and seq % ATT_TILE == 0 and meta_tokens.shape == (N_META, D_MODEL)
    tok_rows = batch * seq
    rows = tok_rows + META_ROWS
    tm = ROW_TILE
    tm_proj = _pick(rows, (3328, 1280, 1024, 768, 512, 256))

    n_tok_tiles = tok_rows // tm
    h_tok = x.reshape(tok_rows, d)
    h_meta = jnp.concatenate([meta_tokens.astype(x.dtype), jnp.zeros((META_ROWS - N_META, d), x.dtype)], axis=0)
    meta_block = 0
    wuq = jnp.pad(mla_w_uq.reshape(depth, MLA_Q_RANK, MLA_HEADS, HEAD + MLA_ROPE),
                  ((0, 0), (0, 0), (0, 0), (0, LANES - HEAD - MLA_ROPE))).reshape(depth, MLA_Q_RANK, -1).astype(BF16)
    wukv = mla_w_ukv.reshape(depth, MLA_KV_RANK, MLA_HEADS, 2 * HEAD)
    wuk = jnp.pad(wukv[..., :HEAD], ((0, 0), (0, 0), (0, 0), (0, LANES - HEAD))).reshape(depth, MLA_KV_RANK, -1).astype(BF16)
    wuv = wukv[..., HEAD:].reshape(depth, MLA_KV_RANK, -1).astype(BF16)
    tabs = _rope_tables(seq)
    row2 = lambda a: a.reshape(1, -1)

    hn = _norm_call(h_tok, h_meta, row2(norm_g[0]), tm)
    out = None
    for l in range(depth):
        last = l == depth - 1
        proj = _proj_call(hn, _perm_w_in(w_in[l]), tm_proj)
        q_m, k_m, v_m, q_d, k_d = _prep_call(proj, row2(mla_cq_g[l]), row2(mla_ckv_g[l]), wuq[l], wuk[l], wuv[l],
                                             tabs, tm, tok_rows, seq)
        gw = ATT_GROUPS * LANES
        n_s = 2 * ATT_GROUPS
        o_sb = _attn_call(_sb_kernel, "sb_attn", proj, gw, C_SBQ // gw, proj, gw, C_SBK // gw,
                          proj, C_SBV // gw, [], _sb_scratch(n_s), batch, seq, WIDTH // gw)
        o_mla = _attn_call(_mla_kernel, "mla_attn", q_m, 2 * gw, 0, k_m, 2 * gw, 0, v_m, 0, [],
                           _softmax_scratch(n_s), batch, seq, WIDTH // gw)
        lam_init = 0.8 - 0.6 * math.exp(-0.3 * l)
        o_diff = _attn_call(functools.partial(_diff_kernel, lam_init=lam_init), "diff_attn",
                            q_d, gw, 0, k_d, gw, 0, proj, C_DV // gw,
                            [diff_lambda[l].astype(F32), row2(diff_norm_g[l])], _softmax_scratch(n_s), batch, seq,
                            WIDTH // gw)
        g_next = row2(final_g if last else norm_g[l + 1])
        res = _merge_call(o_sb, o_mla, o_diff, proj, h_tok, h_meta, meta_block, n_tok_tiles, row2(b_gate[l]),
                          w_o_sb[l].astype(BF16), w_o_mla[l].astype(BF16), w_o_diff[l].astype(BF16),
                          w_out[l].astype(BF16), g_next, tm, tok_rows if last else rows, last)
        if last:
            out = res[0]
        else:
            h, hn = res
            h_tok, h_meta, meta_block = h, h, n_tok_tiles
    return out.reshape(batch, seq, d)
```

```python
import functools
import math

import jax
import jax.numpy as jnp
import numpy as np
from jax import lax
from jax.experimental import pallas as pl
from jax.experimental.pallas import tpu as pltpu

F32 = jnp.float32
BF16 = jnp.bfloat16

D_MODEL = 1024
CHUNK = 64
N_META = 16
ROPE_THETA = 500000.0
EPS = 1e-6
LANES = 128
HEAD = 64
WIDTH = 512
MLA_HEADS = 8
MLA_ROPE = 32
MLA_Q_RANK = 384
MLA_KV_RANK = 256
DIFF_ROT = 16

META_ROWS = 256
ROW_TILE = 256
ATT_TILE = 256
ROW_CHUNK = 128
SCORE_LOOKAHEAD = 8
ATT_GROUPS = 4
NEG_BIG = -1e30
LOG2E = 1.4426950408889634
SB_DEAD_RUN = -104.0
VMEM_LIMIT = 56 * 1024 * 1024

C_GATE = 0
C_SBQ = 3072
C_SBK = 3584
C_SBV = 4096
C_SBZ = 4608
C_DQ = 5120
C_DK = 5632
C_DV = 6144
C_DZ = 6656
C_MZ = 7168
C_LAT = 7680
LAT_W = 768
N_PROJ = 8448
PROJ_TN = 768


def _params(sem):
    return pltpu.CompilerParams(dimension_semantics=sem, vmem_limit_bytes=VMEM_LIMIT)


def _rms(x32, g):
    ms = jnp.mean(x32 * x32, axis=-1, keepdims=True)
    return x32 * lax.rsqrt(ms + EPS) * g


def _residual_rows(tok_ref, meta_ref, n_tok_tiles):
    return jnp.where(pl.program_id(0) < n_tok_tiles, tok_ref[...], meta_ref[...])


def _residual_specs(tm, n_tok_tiles, meta_block):
    return [pl.BlockSpec((tm, D_MODEL), lambda i: (jnp.minimum(i, n_tok_tiles - 1), 0)),
            pl.BlockSpec((tm, D_MODEL), lambda i: (meta_block, 0))]


def _norm_kernel(tok_ref, meta_ref, g_ref, o_ref, *, n_tok_tiles):
    o_ref[...] = _rms(_residual_rows(tok_ref, meta_ref, n_tok_tiles), g_ref[...]).astype(o_ref.dtype)


def _norm_call(h_tok, h_meta, g, tm):
    n_tok_tiles = h_tok.shape[0] // tm
    rows = h_tok.shape[0] + h_meta.shape[0]
    return pl.pallas_call(
        functools.partial(_norm_kernel, n_tok_tiles=n_tok_tiles),
        grid=(rows // tm,),
        in_specs=_residual_specs(tm, n_tok_tiles, 0) + [pl.BlockSpec((1, D_MODEL), lambda i: (0, 0))],
        out_specs=pl.BlockSpec((tm, D_MODEL), lambda i: (i, 0)),
        out_shape=jax.ShapeDtypeStruct((rows, D_MODEL), BF16),
        compiler_params=_params(("parallel",)),
        name="norm_in",
    )(h_tok, h_meta, g)


def _proj_kernel(a_ref, w_ref, o_ref):
    o_ref[...] = jnp.dot(a_ref[...], w_ref[...], preferred_element_type=F32).astype(o_ref.dtype)


def _proj_call(hn, w, tm):
    rows = hn.shape[0]
    return pl.pallas_call(
        _proj_kernel,
        grid=(rows // tm, N_PROJ // PROJ_TN),
        in_specs=[pl.BlockSpec((tm, D_MODEL), lambda i, j: (i, 0)),
                  pl.BlockSpec((D_MODEL, PROJ_TN), lambda i, j: (0, j))],
        out_specs=pl.BlockSpec((tm, PROJ_TN), lambda i, j: (i, j)),
        out_shape=jax.ShapeDtypeStruct((rows, N_PROJ), BF16),
        compiler_params=_params(("parallel", "arbitrary")),
        name="in_proj",
    )(hn, w)


def _rot(x, c, s1, s2, shift):
    return x * c + pltpu.roll(x, LANES - shift, 1) * s1 + pltpu.roll(x, shift, 1) * s2


def _prep_kernel(lat_ref, dq_ref, dk_ref, gq_ref, gkv_ref, wuq_ref, wuk_ref, wuv_ref,
                 cm_ref, s1m_ref, s2m_ref, cd_ref, s1d_ref, s2d_ref,
                 qm_ref, km_ref, vm_ref, dqo_ref, dko_ref):
    lat = lat_ref[...].astype(F32)
    ckv = lat[:, :MLA_KV_RANK]
    cq = lat[:, MLA_KV_RANK:MLA_KV_RANK + MLA_Q_RANK]
    kr = lat[:, MLA_KV_RANK + MLA_Q_RANK:]
    ncq = _rms(cq, gq_ref[...]).astype(BF16)
    nckv = _rms(ckv, gkv_ref[...]).astype(BF16)
    cm, s1m, s2m = cm_ref[...], s1m_ref[...], s2m_ref[...]
    scale_b = LOG2E / math.sqrt(HEAD + MLA_ROPE)
    k_rope = _rot(kr, cm, s1m, s2m, MLA_ROPE // 2)
    vm_ref[...] = jnp.dot(nckv, wuv_ref[...], preferred_element_type=F32).astype(BF16)
    for h in range(MLA_HEADS):
        sl = slice(h * LANES, (h + 1) * LANES)
        qf = jnp.dot(ncq, wuq_ref[:, sl], preferred_element_type=F32)
        qm_ref[:, sl] = (_rot(qf, cm, s1m, s2m, MLA_ROPE // 2) * scale_b).astype(BF16)
        kf = jnp.dot(nckv, wuk_ref[:, sl], preferred_element_type=F32)
        km_ref[:, sl] = (kf + k_rope).astype(BF16)
    cd, s1d, s2d = cd_ref[...], s1d_ref[...], s2d_ref[...]
    for h in range(WIDTH // LANES):
        sl = slice(h * LANES, (h + 1) * LANES)
        dqo_ref[:, sl] = (_rot(dq_ref[:, sl].astype(F32), cd, s1d, s2d, DIFF_ROT // 2) * (LOG2E / math.sqrt(HEAD))).astype(BF16)
        dko_ref[:, sl] = _rot(dk_ref[:, sl].astype(F32), cd, s1d, s2d, DIFF_ROT // 2).astype(BF16)


def _prep_call(proj, gq, gkv, wuq, wuk, wuv, tabs, tm, tok_rows, seq):
    rows = proj.shape[0]
    n_tok_tiles = tok_rows // tm
    per_seq = seq // tm

    def tab_map(i):
        return (jnp.where(i < n_tok_tiles, i % per_seq, per_seq), 0)

    row = lambda w, c: pl.BlockSpec((tm, w), lambda i: (i, c))
    full = lambda a: pl.BlockSpec(a.shape, lambda i: (0, 0))
    tab = pl.BlockSpec((tm, LANES), tab_map)
    out = lambda w: jax.ShapeDtypeStruct((rows, w), BF16)
    return pl.pallas_call(
        _prep_kernel,
        grid=(rows // tm,),
        in_specs=[row(LAT_W, C_LAT // LAT_W), row(WIDTH, C_DQ // WIDTH), row(WIDTH, C_DK // WIDTH),
                  full(gq), full(gkv), full(wuq), full(wuk), full(wuv)] + [tab] * 6,
        out_specs=[row(MLA_HEADS * LANES, 0), row(MLA_HEADS * LANES, 0), row(WIDTH, 0),
                   row(WIDTH, 0), row(WIDTH, 0)],
        out_shape=[out(MLA_HEADS * LANES), out(MLA_HEADS * LANES), out(WIDTH), out(WIDTH), out(WIDTH)],
        compiler_params=_params(("parallel",)),
        name="prep",
    )(proj, proj, proj, gq, gkv, wuq, wuk, wuv, *tabs)


def _sigmoid(x):
    return 0.5 * jnp.tanh(0.5 * x) + 0.5


def _merge_kernel(osb_ref, omla_ref, odiff_ref, zsb_ref, zmla_ref, zdiff_ref, g0_ref, g1_ref, g2_ref,
                  htok_ref, hmeta_ref, bg_ref, wsb_ref, wmla_ref, wdiff_ref, wout_ref, gn_ref, *out_refs,
                  last, n_tok_tiles):
    def branch(o_ref, z_ref, w_ref):
        z = z_ref[...].astype(F32)
        a = o_ref[...].astype(F32) * (z * _sigmoid(z))
        return jnp.dot(a.astype(BF16), w_ref[...], preferred_element_type=F32)

    bg = bg_ref[...]
    merged = _sigmoid(g0_ref[...].astype(F32) + bg[:, :D_MODEL]) * branch(osb_ref, zsb_ref, wsb_ref)
    merged += _sigmoid(g1_ref[...].astype(F32) + bg[:, D_MODEL:2 * D_MODEL]) * branch(omla_ref, zmla_ref, wmla_ref)
    merged += _sigmoid(g2_ref[...].astype(F32) + bg[:, 2 * D_MODEL:]) * branch(odiff_ref, zdiff_ref, wdiff_ref)
    h_old = _residual_rows(htok_ref, hmeta_ref, n_tok_tiles)
    h_new = h_old + jnp.dot(merged.astype(BF16), wout_ref[...], preferred_element_type=F32)
    normed = _rms(h_new, gn_ref[...])
    if last:
        out_refs[0][...] = normed
    else:
        out_refs[0][...] = h_new
        out_refs[1][...] = normed.astype(BF16)


def _merge_call(o_sb, o_mla, o_diff, proj, h_tok, h_meta, meta_block, n_tok_tiles, bg, w_sb, w_mla, w_diff, w_out,
                g_next, tm, out_rows, last):
    row = lambda w, c: pl.BlockSpec((tm, w), lambda i: (i, c))
    full = lambda a: pl.BlockSpec(a.shape, lambda i: (0, 0))
    if last:
        out_specs = [row(D_MODEL, 0)]
        out_shape = [jax.ShapeDtypeStruct((out_rows, D_MODEL), F32)]
    else:
        out_specs = [row(D_MODEL, 0), row(D_MODEL, 0)]
        out_shape = [jax.ShapeDtypeStruct((out_rows, D_MODEL), F32),
                     jax.ShapeDtypeStruct((out_rows, D_MODEL), BF16)]
    return pl.pallas_call(
        functools.partial(_merge_kernel, last=last, n_tok_tiles=n_tok_tiles),
        grid=(out_rows // tm,),
        in_specs=[row(WIDTH, 0), row(WIDTH, 0), row(WIDTH, 0),
                  row(WIDTH, C_SBZ // WIDTH), row(WIDTH, C_MZ // WIDTH), row(WIDTH, C_DZ // WIDTH),
                  row(D_MODEL, 0), row(D_MODEL, 1), row(D_MODEL, 2)]
                 + _residual_specs(tm, n_tok_tiles, meta_block)
                 + [full(bg), full(w_sb), full(w_mla), full(w_diff), full(w_out), full(g_next)],
        out_specs=out_specs,
        out_shape=out_shape,
        compiler_params=_params(("parallel",)),
        name="merge_last" if last else "merge",
    )(o_sb, o_mla, o_diff, proj, proj, proj, proj, proj, proj, h_tok, h_meta, bg, w_sb, w_mla, w_diff, w_out, g_next)


def _qk(q, k):
    return lax.dot_general(q, k, (((1,), (1,)), ((), ())), preferred_element_type=F32)


def _iota2(shape, axis):
    return lax.broadcasted_iota(jnp.int32, shape, axis)


def _half_mask(rows, half):
    lane = _iota2((rows, LANES), 1)
    return (lane >= HEAD) if half else (lane < HEAD)


def _pv(p, v):
    return jnp.dot(p.astype(BF16), v, preferred_element_type=F32)


def _incl_tri(n):
    return (_iota2((n, n), 0) >= _iota2((n, n), 1)).astype(BF16)


def _sb_logw(z, tri):
    nz = -z
    lk = jnp.minimum(nz, 0.0) - jnp.log(1.0 + jnp.exp(jnp.minimum(z, nz)))
    cum = jnp.dot(lk.astype(BF16), tri, preferred_element_type=F32)
    return z + cum, cum[:, 0:1]


def _wide(x, width):
    return jnp.concatenate([x] * (width // LANES), axis=-1)


def _row_blocks(t):
    return [pl.ds(r * ROW_CHUNK, ROW_CHUNK) for r in range(t // ROW_CHUNK)]


def _stream_cols(n_streams):
    return [slice((n // 2) * LANES, (n // 2 + 1) * LANES) for n in range(n_streams)]


def _half_queries(q_ref, t, scale):
    qs = []
    for n, c in enumerate(_stream_cols(2 * (q_ref.shape[1] // LANES))):
        qg = q_ref[:, c] if scale is None else q_ref[:, c] * scale
        qs.append(jnp.where(_half_mask(t, n % 2), qg, jnp.zeros_like(qg)))
    return qs


def _join_halves(o_ref, outs, t):
    for g in range(len(outs) // 2):
        o_ref[:, g * LANES:(g + 1) * LANES] = jnp.where(_half_mask(t, 0), outs[2 * g], outs[2 * g + 1]).astype(o_ref.dtype)


def _sb_kernel(q_ref, k_ref, v_ref, km_ref, vm_ref, o_ref, acc_scr, run_scr, *, n_tok_tiles):
    i = pl.program_id(2)
    t = ATT_TILE
    qs = _half_queries(q_ref, t, 0.125)
    n_s = len(qs)
    cols = _stream_cols(n_s)
    tri = _incl_tri(t)
    tri_m = _incl_tri(N_META)
    kms = [km_ref[:, c] for c in cols]
    vms = [vm_ref[:, c] for c in cols]
    blocks = _row_blocks(t)

    def key_step(kbs, vbs, tri_k, masks):
        def scores(n):
            return _qk(qs[n], kbs[n])

        def log_weights(n, z):
            parts = []
            for b in range(len(blocks)):
                zb = z[b * ROW_CHUNK:(b + 1) * ROW_CHUNK]
                if masks is not None:
                    zb = jnp.where(masks[b], zb, NEG_BIG)
                parts.append(_sb_logw(zb, tri_k))
            return parts

        def fold(n, parts):
            runs = []
            for r, (part, total) in zip(blocks, parts):
                run = run_scr[n, r, :]
                w = jnp.exp(part + (_wide(run, part.shape[1]) if part.shape[1] > LANES else run[:, :part.shape[1]]))
                acc_scr[n, r, :] += _pv(w, vbs[n])
                run_scr[n, r, :] = run + total
                runs.append(run + total)
            return runs

        z = {0: scores(0)}
        if n_s > 1:
            z[1] = scores(1)
        lw = {0: log_weights(0, z.pop(0))}
        runs = []
        for n in range(n_s):
            if n + 2 < n_s:
                z[n + 2] = scores(n + 2)
            if n + 1 < n_s:
                lw[n + 1] = log_weights(n + 1, z.pop(n + 1))
            runs += fold(n, lw.pop(n))
        return jnp.max(functools.reduce(jnp.maximum, runs))

    def tiles_at(ref, p):
        rows = pl.ds(pl.multiple_of((i - p) * t, t), t)
        per_group = [ref[rows, cols[2 * g]] for g in range(n_s // 2)]
        return [per_group[n // 2] for n in range(n_s)]

    def tok():
        causal = [_iota2((ROW_CHUNK, t), 1) < _iota2((ROW_CHUNK, t), 0) + b * ROW_CHUNK for b in range(len(blocks))]
        acc_scr[...] = jnp.zeros_like(acc_scr)
        run_scr[...] = jnp.zeros_like(run_scr)
        first = key_step(tiles_at(k_ref, 0), tiles_at(v_ref, 0), tri, causal)

        def alive(c):
            return jnp.logical_and(c[0] <= i, c[1] >= SB_DEAD_RUN)

        def older(c):
            p = c[0]
            return p + 1, key_step(tiles_at(k_ref, p), tiles_at(v_ref, p), tri, None)

        _, largest = lax.while_loop(alive, older, (jnp.int32(1), first))

        @pl.when(largest >= SB_DEAD_RUN)
        def _():
            key_step(kms, vms, tri_m, None)

        return [acc_scr[n] for n in range(n_s)]

    def meta():
        causal_m = _iota2((t, N_META), 1) < _iota2((t, N_META), 0)
        zm = [_qk(qh, km) for qh, km in zip(qs, kms)]
        return [_pv(jnp.exp(_sb_logw(jnp.where(causal_m, z, NEG_BIG), tri_m)[0]), vm) for z, vm in zip(zm, vms)]

    _join_halves(o_ref, lax.cond(i < n_tok_tiles, tok, meta), t)


def _sb_scratch(n_s):
    t = ATT_TILE
    return [pltpu.VMEM((n_s, t, LANES), F32), pltpu.VMEM((n_s, t, LANES), F32)]


def _row_max(s):
    return jnp.max(s, axis=-1, keepdims=True)


def _row_sum(p):
    return jnp.sum(p, axis=-1, keepdims=True)


def _softmax_attend(i, n_tok_tiles, qs, k_tile, v_tile, kms, vms, acc_scr):
    t = ATT_TILE
    n_s = len(qs)

    def scores_t(keys, q):
        return lax.dot_general(keys, q, (((1,), (1,)), ((), ())), preferred_element_type=F32)

    def pv_t(v, p_t):
        return lax.dot_general(v, p_t.astype(BF16), (((0,), (0,)), ((), ())), preferred_element_type=F32)

    def col_max(s):
        return jnp.max(s, axis=0, keepdims=True)

    def col_sum(p):
        return jnp.sum(p, axis=0, keepdims=True)

    streams = range(n_s)

    def sweep(start, width, stats):
        ss = {n: scores_t(k_tile(n, start, width), qs[n]) for n in range(min(SCORE_LOOKAHEAD, n_s))}
        out, updates = [], []
        for n in streams:
            s = ss.pop(n)
            m_new = jnp.maximum(stats[2 * n], col_max(s))
            alpha = jnp.exp2(stats[2 * n] - m_new)
            p = jnp.exp2(s - m_new)
            out += [m_new, alpha * stats[2 * n + 1] + col_sum(p)]
            updates.append((alpha, pv_t(v_tile(n, start, width), p)))
            if n + SCORE_LOOKAHEAD < n_s:
                ss[n + SCORE_LOOKAHEAD] = scores_t(k_tile(n + SCORE_LOOKAHEAD, start, width), qs[n + SCORE_LOOKAHEAD])
        for n, (alpha, pv) in enumerate(updates):
            acc_scr[n] = alpha * acc_scr[n] + pv
        return tuple(out)

    def tok():
        def first(before):
            start = pl.multiple_of((i - before) * t, (1 + before) * t)
            width = (1 + before) * t
            chunk_ok = _iota2((width, t), 0) // CHUNK <= _iota2((width, t), 1) // CHUNK + before * (t // CHUNK)
            sds = [jnp.where(chunk_ok, scores_t(k_tile(n, start, width), qs[n]), NEG_BIG) for n in streams]
            sms = [scores_t(kms[n], qs[n]) for n in streams]
            ms = [jnp.maximum(col_max(sds[n]), col_max(sms[n])) for n in streams]
            pds = [jnp.exp2(sds[n] - ms[n]) for n in streams]
            pms = [jnp.exp2(sms[n] - ms[n]) for n in streams]
            ls = [col_sum(pds[n]) + col_sum(pms[n]) for n in streams]
            for n in streams:
                acc_scr[n] = pv_t(v_tile(n, start, width), pds[n]) + pv_t(vms[n], pms[n])
            return tuple(x for n in streams for x in (ms[n], ls[n]))

        stats = lax.cond(i % 2 == 1, lambda: first(1), lambda: first(0))
        stats = lax.fori_loop(0, i // 2, lambda j, st: sweep(pl.multiple_of(j * 2 * t, 2 * t), 2 * t, st), stats)
        return [(acc_scr[n] * (1.0 / stats[2 * n + 1])).T for n in range(n_s)]

    def meta():
        sms = [_qk(q, km) for q, km in zip(qs, kms)]
        pms = [jnp.exp2(sm - _row_max(sm)) for sm in sms]
        return [_pv(pm, vm) * (1.0 / _row_sum(pm)) for pm, vm in zip(pms, vms)]

    return lax.cond(i < n_tok_tiles, tok, meta)


def _softmax_scratch(n_s):
    t = ATT_TILE
    return [pltpu.VMEM((n_s, LANES, t), F32)]


def _mla_kernel(q_ref, k_ref, v_ref, km_ref, vm_ref, o_ref, *scratch, n_tok_tiles):
    i = pl.program_id(2)
    n_s = q_ref.shape[1] // LANES
    qcols = [slice(n * LANES, (n + 1) * LANES) for n in range(n_s)]
    vcols = _stream_cols(n_s)
    outs = _softmax_attend(
        i, n_tok_tiles, [q_ref[:, c] for c in qcols],
        lambda n, st, w: k_ref[pl.ds(st, w), qcols[n]], lambda n, st, w: v_ref[pl.ds(st, w), vcols[n]],
        [km_ref[:, c] for c in qcols], [vm_ref[:, c] for c in vcols], *scratch)
    _join_halves(o_ref, outs, ATT_TILE)


def _diff_kernel(q_ref, k_ref, v_ref, km_ref, vm_ref, lam_ref, g_ref, o_ref, *scratch, n_tok_tiles, lam_init):
    i = pl.program_id(2)
    qs = _half_queries(q_ref, ATT_TILE, None)
    cols = _stream_cols(len(qs))
    outs = _softmax_attend(
        i, n_tok_tiles, qs,
        lambda n, st, w: k_ref[pl.ds(st, w), cols[n]], lambda n, st, w: v_ref[pl.ds(st, w), cols[n]],
        [km_ref[:, c] for c in cols], [vm_ref[:, c] for c in cols], *scratch)
    lam4 = lam_ref[...]
    lam = (jnp.exp(jnp.sum(lam4[0:1] * lam4[1:2], axis=-1, keepdims=True))
           - jnp.exp(jnp.sum(lam4[2:3] * lam4[3:4], axis=-1, keepdims=True)) + lam_init)
    for g in range(len(qs) // 2):
        o = outs[2 * g] - lam * outs[2 * g + 1]
        o_ref[:, g * LANES:(g + 1) * LANES] = (_rms(o, g_ref[...]) * (1.0 - lam_init)).astype(o_ref.dtype)


def _attn_call(kernel, name, q, qw, qc, k, kw, kc, v, vc, extra, scratch, batch, seq, groups):
    rows = q.shape[0]
    t = ATT_TILE
    n_tok_tiles = seq // t
    tok_rows = batch * seq
    meta_q = tok_rows // t
    meta_k = tok_rows // N_META
    vw = WIDTH // groups

    def q_map(b, g, i):
        return (jnp.where(i < n_tok_tiles, b * n_tok_tiles + i, meta_q), qc + g)

    def o_map(b, g, i):
        return (jnp.where(i < n_tok_tiles, b * n_tok_tiles + i, meta_q + b), g)

    in_specs = [
        pl.BlockSpec((t, qw), q_map),
        pl.BlockSpec((seq, kw), lambda b, g, i: (b, kc + g)),
        pl.BlockSpec((seq, vw), lambda b, g, i: (b, vc + g)),
        pl.BlockSpec((N_META, kw), lambda b, g, i: (meta_k, kc + g)),
        pl.BlockSpec((N_META, vw), lambda b, g, i: (meta_k, vc + g)),
    ] + [pl.BlockSpec(a.shape, lambda b, g, i: (0, 0)) for a in extra]
    return pl.pallas_call(
        functools.partial(kernel, n_tok_tiles=n_tok_tiles),
        grid=(batch, groups, n_tok_tiles + 1),
        in_specs=in_specs,
        out_specs=pl.BlockSpec((t, vw), o_map),
        out_shape=jax.ShapeDtypeStruct((rows + (batch - 1) * t, WIDTH), BF16),
        scratch_shapes=scratch,
        compiler_params=_params(("parallel", "parallel", "arbitrary")),
        name=name,
    )(q, k, v, k, v, *extra)


def _perm_w_in(w_in):
    sizes = [512, 512, 512, 512, 384, 256, 32, 512, 512, 512, 512, 512, 3072]
    offs = [0]
    for s in sizes:
        offs.append(offs[-1] + s)
    seg = [w_in[..., offs[n]:offs[n + 1]] for n in range(len(sizes))]
    sb_q, sb_k, sb_v, sb_z, cq, ckv, kr, mz, dq, dk, dv, dz, gate = seg
    pad = lambda n: jnp.zeros(w_in.shape[:-1] + (n,), w_in.dtype)
    cols = [gate, sb_q, sb_k, sb_v, sb_z, dq, dk, dv, dz, mz, ckv, cq, pad(HEAD), kr, pad(LANES - HEAD - MLA_ROPE)]
    return jnp.concatenate(cols, axis=-1).astype(BF16)


def _rope_tables(seq):
    pos = np.concatenate([np.arange(seq) + N_META, np.arange(META_ROWS)]).astype(np.float64)[:, None]

    def tables(dim, starts):
        inv = ROPE_THETA ** (-np.arange(0, dim, 2, dtype=np.float64) / dim)
        ang = pos * inv[None, :]
        cos, sin = np.cos(ang), np.sin(ang)
        half = dim // 2
        c = np.ones((pos.shape[0], LANES))
        s1 = np.zeros((pos.shape[0], LANES))
        s2 = np.zeros((pos.shape[0], LANES))
        for st in starts:
            c[:, st:st + half] = cos
            c[:, st + half:st + dim] = cos
            s1[:, st:st + half] = -sin
            s2[:, st + half:st + dim] = sin
        return [jnp.asarray(a, F32) for a in (c, s1, s2)]

    return tables(MLA_ROPE, [HEAD]) + tables(DIFF_ROT, [0, HEAD])


def _pick(rows, candidates):
    for c in candidates:
        if rows % c == 0:
            return c
    raise ValueError(f"no row tile for {rows}")


def kernel(x, meta_tokens, norm_g, w_in, b_gate, mla_cq_g, mla_ckv_g, mla_w_uq, mla_w_ukv, diff_lambda,
           diff_norm_g, w_o_sb, w_o_mla, w_o_diff, w_out, final_g):
    batch, seq, d = x.shape
    depth = norm_g.shape[0]
    assert d == D_MODEL and seq % ATT_TILE == 0 and meta_tokens.shape == (N_META, D_MODEL)
    tok_rows = batch * seq
    rows = tok_rows + META_ROWS
    tm = ROW_TILE
    tm_proj = _pick(rows, (3328, 1280, 1024, 768, 512, 256))

    n_tok_tiles = tok_rows // tm
    h_tok = x.reshape(tok_rows, d)
    h_meta = jnp.concatenate([meta_tokens.astype(x.dtype), jnp.zeros((META_ROWS - N_META, d), x.dtype)], axis=0)
    meta_block = 0
    wuq = jnp.pad(mla_w_uq.reshape(depth, MLA_Q_RANK, MLA_HEADS, HEAD + MLA_ROPE),
                  ((0, 0), (0, 0), (0, 0), (0, LANES - HEAD - MLA_ROPE))).reshape(depth, MLA_Q_RANK, -1).astype(BF16)
    wukv = mla_w_ukv.reshape(depth, MLA_KV_RANK, MLA_HEADS, 2 * HEAD)
    wuk = jnp.pad(wukv[..., :HEAD], ((0, 0), (0, 0), (0, 0), (0, LANES - HEAD))).reshape(depth, MLA_KV_RANK, -1).astype(BF16)
    wuv = wukv[..., HEAD:].reshape(depth, MLA_KV_RANK, -1).astype(BF16)
    tabs = _rope_tables(seq)
    row2 = lambda a: a.reshape(1, -1)

    hn = _norm_call(h_tok, h_meta, row2(norm_g[0]), tm)
    out = None
    for l in range(depth):
        last = l == depth - 1
        proj = _proj_call(hn, _perm_w_in(w_in[l]), tm_proj)
        q_m, k_m, v_m, q_d, k_d = _prep_call(proj, row2(mla_cq_g[l]), row2(mla_ckv_g[l]), wuq[l], wuk[l], wuv[l],
                                             tabs, tm, tok_rows, seq)
        gw = ATT_GROUPS * LANES
        n_s = 2 * ATT_GROUPS
        o_sb = _attn_call(_sb_kernel, "sb_attn", proj, gw, C_SBQ // gw, proj, gw, C_SBK // gw,
                          proj, C_SBV // gw, [], _sb_scratch(n_s), batch, seq, WIDTH // gw)
        o_mla = _attn_call(_mla_kernel, "mla_attn", q_m, 2 * gw, 0, k_m, 2 * gw, 0, v_m, 0, [],
                           _softmax_scratch(n_s), batch, seq, WIDTH // gw)
        lam_init = 0.8 - 0.6 * math.exp(-0.3 * l)
        o_diff = _attn_call(functools.partial(_diff_kernel, lam_init=lam_init), "diff_attn",
                            q_d, gw, 0, k_d, gw, 0, proj, C_DV // gw,
                            [diff_lambda[l].astype(F32), row2(diff_norm_g[l])], _softmax_scratch(n_s), batch, seq,
                            WIDTH // gw)
        g_next = row2(final_g if last else norm_g[l + 1])
        res = _merge_call(o_sb, o_mla, o_diff, proj, h_tok, h_meta, meta_block, n_tok_tiles, row2(b_gate[l]),
                          w_o_sb[l].astype(BF16), w_o_mla[l].astype(BF16), w_o_diff[l].astype(BF16),
                          w_out[l].astype(BF16), g_next, tm, tok_rows if last else rows, last)
        if last:
            out = res[0]
        else:
            h, hn = res
            h_tok, h_meta, meta_block = h, h, n_tok_tiles
    return out.reshape(batch, seq, d)
```

```python
import functools
import math

import jax
import jax.numpy as jnp
import numpy as np
from jax import lax
from jax.experimental import pallas as pl
from jax.experimental.pallas import tpu as pltpu

F32 = jnp.float32
BF16 = jnp.bfloat16

D_MODEL = 1024
CHUNK = 64
N_META = 16
ROPE_THETA = 500000.0
EPS = 1e-6
LANES = 128
HEAD = 64
WIDTH = 512
MLA_HEADS = 8
MLA_ROPE = 32
MLA_Q_RANK = 384
MLA_KV_RANK = 256
DIFF_ROT = 16

META_ROWS = 256
ROW_TILE = 256
ATT_TILE = 256
ROW_CHUNK = 128
SCORE_LOOKAHEAD = 8
ATT_GROUPS = 4
NEG_BIG = -1e30
LOG2E = 1.4426950408889634
SB_DEAD_RUN = -104.0
VMEM_LIMIT = 56 * 1024 * 1024

C_GATE = 0
C_SBQ = 3072
C_SBK = 3584
C_SBV = 4096
C_SBZ = 4608
C_DQ = 5120
C_DK = 5632
C_DV = 6144
C_DZ = 6656
C_MZ = 7168
C_LAT = 7680
LAT_W = 768
N_PROJ = 8448
PROJ_TN = 768


def _params(sem):
    return pltpu.CompilerParams(dimension_semantics=sem, vmem_limit_bytes=VMEM_LIMIT)


def _rms(x32, g):
    ms = jnp.mean(x32 * x32, axis=-1, keepdims=True)
    return x32 * lax.rsqrt(ms + EPS) * g


def _residual_rows(tok_ref, meta_ref, n_tok_tiles):
    return jnp.where(pl.program_id(0) < n_tok_tiles, tok_ref[...], meta_ref[...])


def _residual_specs(tm, n_tok_tiles, meta_block):
    return [pl.BlockSpec((tm, D_MODEL), lambda i: (jnp.minimum(i, n_tok_tiles - 1), 0)),
            pl.BlockSpec((tm, D_MODEL), lambda i: (meta_block, 0))]


def _norm_kernel(tok_ref, meta_ref, g_ref, o_ref, *, n_tok_tiles):
    o_ref[...] = _rms(_residual_rows(tok_ref, meta_ref, n_tok_tiles), g_ref[...]).astype(o_ref.dtype)


def _norm_call(h_tok, h_meta, g, tm):
    n_tok_tiles = h_tok.shape[0] // tm
    rows = h_tok.shape[0] + h_meta.shape[0]
    return pl.pallas_call(
        functools.partial(_norm_kernel, n_tok_tiles=n_tok_tiles),
        grid=(rows // tm,),
        in_specs=_residual_specs(tm, n_tok_tiles, 0) + [pl.BlockSpec((1, D_MODEL), lambda i: (0, 0))],
        out_specs=pl.BlockSpec((tm, D_MODEL), lambda i: (i, 0)),
        out_shape=jax.ShapeDtypeStruct((rows, D_MODEL), BF16),
        compiler_params=_params(("parallel",)),
        name="norm_in",
    )(h_tok, h_meta, g)


def _proj_kernel(a_ref, w_ref, o_ref):
    o_ref[...] = jnp.dot(a_ref[...], w_ref[...], preferred_element_type=F32).astype(o_ref.dtype)


def _proj_call(hn, w, tm):
    rows = hn.shape[0]
    return pl.pallas_call(
        _proj_kernel,
        grid=(rows // tm, N_PROJ // PROJ_TN),
        in_specs=[pl.BlockSpec((tm, D_MODEL), lambda i, j: (i, 0)),
                  pl.BlockSpec((D_MODEL, PROJ_TN), lambda i, j: (0, j))],
        out_specs=pl.BlockSpec((tm, PROJ_TN), lambda i, j: (i, j)),
        out_shape=jax.ShapeDtypeStruct((rows, N_PROJ), BF16),
        compiler_params=_params(("parallel", "arbitrary")),
        name="in_proj",
    )(hn, w)


def _rot(x, c, s1, s2, shift):
    return x * c + pltpu.roll(x, LANES - shift, 1) * s1 + pltpu.roll(x, shift, 1) * s2


def _prep_kernel(lat_ref, dq_ref, dk_ref, gq_ref, gkv_ref, wuq_ref, wuk_ref, wuv_ref,
                 cm_ref, s1m_ref, s2m_ref, cd_ref, s1d_ref, s2d_ref,
                 qm_ref, km_ref, vm_ref, dqo_ref, dko_ref):
    lat = lat_ref[...].astype(F32)
    ckv = lat[:, :MLA_KV_RANK]
    cq = lat[:, MLA_KV_RANK:MLA_KV_RANK + MLA_Q_RANK]
    kr = lat[:, MLA_KV_RANK + MLA_Q_RANK:]
    ncq = _rms(cq, gq_ref[...]).astype(BF16)
    nckv = _rms(ckv, gkv_ref[...]).astype(BF16)
    cm, s1m, s2m = cm_ref[...], s1m_ref[...], s2m_ref[...]
    scale_b = LOG2E / math.sqrt(HEAD + MLA_ROPE)
    k_rope = _rot(kr, cm, s1m, s2m, MLA_ROPE // 2)
    vm_ref[...] = jnp.dot(nckv, wuv_ref[...], preferred_element_type=F32).astype(BF16)
    for h in range(MLA_HEADS):
        sl = slice(h * LANES, (h + 1) * LANES)
        qf = jnp.dot(ncq, wuq_ref[:, sl], preferred_element_type=F32)
        qm_ref[:, sl] = (_rot(qf, cm, s1m, s2m, MLA_ROPE // 2) * scale_b).astype(BF16)
        kf = jnp.dot(nckv, wuk_ref[:, sl], preferred_element_type=F32)
        km_ref[:, sl] = (kf + k_rope).astype(BF16)
    cd, s1d, s2d = cd_ref[...], s1d_ref[...], s2d_ref[...]
    for h in range(WIDTH // LANES):
        sl = slice(h * LANES, (h + 1) * LANES)
        dqo_ref[:, sl] = (_rot(dq_ref[:, sl].astype(F32), cd, s1d, s2d, DIFF_ROT // 2) * (LOG2E / math.sqrt(HEAD))).astype(BF16)
        dko_ref[:, sl] = _rot(dk_ref[:, sl].astype(F32), cd, s1d, s2d, DIFF_ROT // 2).astype(BF16)


def _prep_call(proj, gq, gkv, wuq, wuk, wuv, tabs, tm, tok_rows, seq):
    rows = proj.shape[0]
    n_tok_tiles = tok_rows // tm
    per_seq = seq // tm

    def tab_map(i):
        return (jnp.where(i < n_tok_tiles, i % per_seq, per_seq), 0)

    row = lambda w, c: pl.BlockSpec((tm, w), lambda i: (i, c))
    full = lambda a: pl.BlockSpec(a.shape, lambda i: (0, 0))
    tab = pl.BlockSpec((tm, LANES), tab_map)
    out = lambda w: jax.ShapeDtypeStruct((rows, w), BF16)
    return pl.pallas_call(
        _prep_kernel,
        grid=(rows // tm,),
        in_specs=[row(LAT_W, C_LAT // LAT_W), row(WIDTH, C_DQ // WIDTH), row(WIDTH, C_DK // WIDTH),
                  full(gq), full(gkv), full(wuq), full(wuk), full(wuv)] + [tab] * 6,
        out_specs=[row(MLA_HEADS * LANES, 0), row(MLA_HEADS * LANES, 0), row(WIDTH, 0),
                   row(WIDTH, 0), row(WIDTH, 0)],
        out_shape=[out(MLA_HEADS * LANES), out(MLA_HEADS * LANES), out(WIDTH), out(WIDTH), out(WIDTH)],
        compiler_params=_params(("parallel",)),
        name="prep",
    )(proj, proj, proj, gq, gkv, wuq, wuk, wuv, *tabs)


def _sigmoid(x):
    return 0.5 * jnp.tanh(0.5 * x) + 0.5


def _merge_kernel(osb_ref, omla_ref, odiff_ref, zsb_ref, zmla_ref, zdiff_ref, g0_ref, g1_ref, g2_ref,
                  htok_ref, hmeta_ref, bg_ref, wsb_ref, wmla_ref, wdiff_ref, wout_ref, gn_ref, *out_refs,
                  last, n_tok_tiles):
    def branch(o_ref, z_ref, w_ref):
        z = z_ref[...].astype(F32)
        a = o_ref[...].astype(F32) * (z * _sigmoid(z))
        return jnp.dot(a.astype(BF16), w_ref[...], preferred_element_type=F32)

    bg = bg_ref[...]

    def twice_gate(g_ref, n):
        return 1.0 + jnp.tanh(0.5 * (g_ref[...].astype(F32) + bg[:, n * D_MODEL:(n + 1) * D_MODEL]))

    merged = twice_gate(g0_ref, 0) * branch(osb_ref, zsb_ref, wsb_ref)
    merged += twice_gate(g1_ref, 1) * branch(omla_ref, zmla_ref, wmla_ref)
    merged += twice_gate(g2_ref, 2) * branch(odiff_ref, zdiff_ref, wdiff_ref)
    merged *= 0.5
    h_old = _residual_rows(htok_ref, hmeta_ref, n_tok_tiles)
    h_new = h_old + jnp.dot(merged.astype(BF16), wout_ref[...], preferred_element_type=F32)
    normed = _rms(h_new, gn_ref[...])
    if last:
        out_refs[0][...] = normed
    else:
        out_refs[0][...] = h_new
        out_refs[1][...] = normed.astype(BF16)


def _merge_call(o_sb, o_mla, o_diff, proj, h_tok, h_meta, meta_block, n_tok_tiles, bg, w_sb, w_mla, w_diff, w_out,
                g_next, tm, out_rows, last):
    row = lambda w, c: pl.BlockSpec((tm, w), lambda i: (i, c))
    full = lambda a: pl.BlockSpec(a.shape, lambda i: (0, 0))
    if last:
        out_specs = [row(D_MODEL, 0)]
        out_shape = [jax.ShapeDtypeStruct((out_rows, D_MODEL), F32)]
    else:
        out_specs = [row(D_MODEL, 0), row(D_MODEL, 0)]
        out_shape = [jax.ShapeDtypeStruct((out_rows, D_MODEL), F32),
                     jax.ShapeDtypeStruct((out_rows, D_MODEL), BF16)]
    return pl.pallas_call(
        functools.partial(_merge_kernel, last=last, n_tok_tiles=n_tok_tiles),
        grid=(out_rows // tm,),
        in_specs=[row(WIDTH, 0), row(WIDTH, 0), row(WIDTH, 0),
                  row(WIDTH, C_SBZ // WIDTH), row(WIDTH, C_MZ // WIDTH), row(WIDTH, C_DZ // WIDTH),
                  row(D_MODEL, 0), row(D_MODEL, 1), row(D_MODEL, 2)]
                 + _residual_specs(tm, n_tok_tiles, meta_block)
                 + [full(bg), full(w_sb), full(w_mla), full(w_diff), full(w_out), full(g_next)],
        out_specs=out_specs,
        out_shape=out_shape,
        compiler_params=_params(("parallel",)),
        name="merge_last" if last else "merge",
    )(o_sb, o_mla, o_diff, proj, proj, proj, proj, proj, proj, h_tok, h_meta, bg, w_sb, w_mla, w_diff, w_out, g_next)


def _qk(q, k):
    return lax.dot_general(q, k, (((1,), (1,)), ((), ())), preferred_element_type=F32)


def _iota2(shape, axis):
    return lax.broadcasted_iota(jnp.int32, shape, axis)


def _half_mask(rows, half):
    lane = _iota2((rows, LANES), 1)
    return (lane >= HEAD) if half else (lane < HEAD)


def _pv(p, v):
    return jnp.dot(p.astype(BF16), v, preferred_element_type=F32)


def _incl_tri(n):
    return (_iota2((n, n), 0) >= _iota2((n, n), 1)).astype(BF16)


def _sb_logw(z, tri):
    nz = -z
    lk = jnp.minimum(nz, 0.0) - jnp.log(1.0 + jnp.exp(jnp.minimum(z, nz)))
    cum = jnp.dot(lk.astype(BF16), tri, preferred_element_type=F32)
    return z + cum, cum[:, 0:1]


def _wide(x, width):
    return jnp.concatenate([x] * (width // LANES), axis=-1)


def _row_blocks(t):
    return [pl.ds(r * ROW_CHUNK, ROW_CHUNK) for r in range(t // ROW_CHUNK)]


def _stream_cols(n_streams):
    return [slice((n // 2) * LANES, (n // 2 + 1) * LANES) for n in range(n_streams)]


def _half_queries(q_ref, t, scale):
    qs = []
    for n, c in enumerate(_stream_cols(2 * (q_ref.shape[1] // LANES))):
        qg = q_ref[:, c] if scale is None else q_ref[:, c] * scale
        qs.append(jnp.where(_half_mask(t, n % 2), qg, jnp.zeros_like(qg)))
    return qs


def _join_halves(o_ref, outs, t):
    for g in range(len(outs) // 2):
        o_ref[:, g * LANES:(g + 1) * LANES] = jnp.where(_half_mask(t, 0), outs[2 * g], outs[2 * g + 1]).astype(o_ref.dtype)


def _sb_kernel(q_ref, k_ref, v_ref, km_ref, vm_ref, o_ref, acc_scr, run_scr):
    i = pl.program_id(2) - 1
    t = ATT_TILE
    qs = _half_queries(q_ref, t, 0.125)
    n_s = len(qs)
    cols = _stream_cols(n_s)
    tri = _incl_tri(t)
    tri_m = _incl_tri(N_META)
    kms = [km_ref[:, c] for c in cols]
    vms = [vm_ref[:, c] for c in cols]
    blocks = _row_blocks(t)

    def key_step(kbs, vbs, tri_k, masks):
        def scores(n):
            return _qk(qs[n], kbs[n])

        def log_weights(n, z):
            parts = []
            for b in range(len(blocks)):
                zb = z[b * ROW_CHUNK:(b + 1) * ROW_CHUNK]
                if masks is not None:
                    zb = jnp.where(masks[b], zb, NEG_BIG)
                parts.append(_sb_logw(zb, tri_k))
            return parts

        def fold(n, parts):
            runs = []
            for r, (part, total) in zip(blocks, parts):
                run = run_scr[n, r, :]
                w = jnp.exp(part + (_wide(run, part.shape[1]) if part.shape[1] > LANES else run[:, :part.shape[1]]))
                acc_scr[n, r, :] += _pv(w, vbs[n])
                run_scr[n, r, :] = run + total
                runs.append(run + total)
            return runs

        z = {0: scores(0)}
        if n_s > 1:
            z[1] = scores(1)
        lw = {0: log_weights(0, z.pop(0))}
        runs = []
        for n in range(n_s):
            if n + 2 < n_s:
                z[n + 2] = scores(n + 2)
            if n + 1 < n_s:
                lw[n + 1] = log_weights(n + 1, z.pop(n + 1))
            runs += fold(n, lw.pop(n))
        return jnp.max(functools.reduce(jnp.maximum, runs))

    def tiles_at(ref, p):
        rows = pl.ds(pl.multiple_of((i - p) * t, t), t)
        per_group = [ref[rows, cols[2 * g]] for g in range(n_s // 2)]
        return [per_group[n // 2] for n in range(n_s)]

    def tok():
        causal = [_iota2((ROW_CHUNK, t), 1) < _iota2((ROW_CHUNK, t), 0) + b * ROW_CHUNK for b in range(len(blocks))]
        acc_scr[...] = jnp.zeros_like(acc_scr)
        run_scr[...] = jnp.zeros_like(run_scr)
        first = key_step(tiles_at(k_ref, 0), tiles_at(v_ref, 0), tri, causal)

        def alive(c):
            return jnp.logical_and(c[0] <= i, c[1] >= SB_DEAD_RUN)

        def older(c):
            p = c[0]
            return p + 1, key_step(tiles_at(k_ref, p), tiles_at(v_ref, p), tri, None)

        _, largest = lax.while_loop(alive, older, (jnp.int32(1), first))

        @pl.when(largest >= SB_DEAD_RUN)
        def _():
            key_step(kms, vms, tri_m, None)

        return [acc_scr[n] for n in range(n_s)]

    def meta():
        causal_m = _iota2((t, N_META), 1) < _iota2((t, N_META), 0)
        zm = [_qk(qh, km) for qh, km in zip(qs, kms)]
        return [_pv(jnp.exp(_sb_logw(jnp.where(causal_m, z, NEG_BIG), tri_m)[0]), vm) for z, vm in zip(zm, vms)]

    _join_halves(o_ref, lax.cond(i >= 0, tok, meta), t)


def _sb_scratch(n_s):
    t = ATT_TILE
    return [pltpu.VMEM((n_s, t, LANES), F32), pltpu.VMEM((n_s, t, LANES), F32)]


def _row_max(s):
    return jnp.max(s, axis=-1, keepdims=True)


def _row_sum(p):
    return jnp.sum(p, axis=-1, keepdims=True)


def _softmax_attend(i, qs, k_tile, v_tile, kms, vms, acc_scr):
    t = ATT_TILE
    n_s = len(qs)

    def scores_t(keys, q):
        return lax.dot_general(keys, q, (((1,), (1,)), ((), ())), preferred_element_type=F32)

    def pv_t(v, p_t):
        return lax.dot_general(v, p_t.astype(BF16), (((0,), (0,)), ((), ())), preferred_element_type=F32)

    def col_max(s):
        return jnp.max(s, axis=0, keepdims=True)

    def col_sum(p):
        return jnp.sum(p, axis=0, keepdims=True)

    streams = range(n_s)

    def sweep(start, width, stats):
        ss = {n: scores_t(k_tile(n, start, width), qs[n]) for n in range(min(SCORE_LOOKAHEAD, n_s))}
        out, updates = [], []
        for n in streams:
            s = ss.pop(n)
            m_new = jnp.maximum(stats[2 * n], col_max(s))
            alpha = jnp.exp2(stats[2 * n] - m_new)
            p = jnp.exp2(s - m_new)
            out += [m_new, alpha * stats[2 * n + 1] + col_sum(p)]
            updates.append((alpha, pv_t(v_tile(n, start, width), p)))
            if n + SCORE_LOOKAHEAD < n_s:
                ss[n + SCORE_LOOKAHEAD] = scores_t(k_tile(n + SCORE_LOOKAHEAD, start, width), qs[n + SCORE_LOOKAHEAD])
        for n, (alpha, pv) in enumerate(updates):
            acc_scr[n] = alpha * acc_scr[n] + pv
        return tuple(out)

    def tok():
        def first(before):
            start = pl.multiple_of((i - before) * t, (1 + before) * t)
            width = (1 + before) * t
            chunk_ok = _iota2((width, t), 0) // CHUNK <= _iota2((width, t), 1) // CHUNK + before * (t // CHUNK)
            sds = [jnp.where(chunk_ok, scores_t(k_tile(n, start, width), qs[n]), NEG_BIG) for n in streams]
            sms = [scores_t(kms[n], qs[n]) for n in streams]
            ms = [jnp.maximum(col_max(sds[n]), col_max(sms[n])) for n in streams]
            pds = [jnp.exp2(sds[n] - ms[n]) for n in streams]
            pms = [jnp.exp2(sms[n] - ms[n]) for n in streams]
            ls = [col_sum(pds[n]) + col_sum(pms[n]) for n in streams]
            for n in streams:
                acc_scr[n] = pv_t(v_tile(n, start, width), pds[n]) + pv_t(vms[n], pms[n])
            return tuple(x for n in streams for x in (ms[n], ls[n]))

        stats = lax.cond(i % 2 == 1, lambda: first(1), lambda: first(0))
        stats = lax.fori_loop(0, i // 2, lambda j, st: sweep(pl.multiple_of(j * 2 * t, 2 * t), 2 * t, st), stats)
        return [(acc_scr[n] * (1.0 / stats[2 * n + 1])).T for n in range(n_s)]

    def meta():
        sms = [_qk(q, km) for q, km in zip(qs, kms)]
        pms = [jnp.exp2(sm - _row_max(sm)) for sm in sms]
        return [_pv(pm, vm) * (1.0 / _row_sum(pm)) for pm, vm in zip(pms, vms)]

    return lax.cond(i >= 0, tok, meta)


def _softmax_scratch(n_s):
    t = ATT_TILE
    return [pltpu.VMEM((n_s, LANES, t), F32)]


def _mla_kernel(q_ref, k_ref, v_ref, km_ref, vm_ref, o_ref, *scratch):
    i = pl.program_id(2) - 1
    n_s = q_ref.shape[1] // LANES
    qcols = [slice(n * LANES, (n + 1) * LANES) for n in range(n_s)]
    vcols = _stream_cols(n_s)
    outs = _softmax_attend(
        i, [q_ref[:, c] for c in qcols],
        lambda n, st, w: k_ref[pl.ds(st, w), qcols[n]], lambda n, st, w: v_ref[pl.ds(st, w), vcols[n]],
        [km_ref[:, c] for c in qcols], [vm_ref[:, c] for c in vcols], *scratch)
    _join_halves(o_ref, outs, ATT_TILE)


def _diff_kernel(q_ref, k_ref, v_ref, km_ref, vm_ref, lam_ref, g_ref, o_ref, *scratch, lam_init):
    i = pl.program_id(2) - 1
    qs = _half_queries(q_ref, ATT_TILE, None)
    cols = _stream_cols(len(qs))
    outs = _softmax_attend(
        i, qs,
        lambda n, st, w: k_ref[pl.ds(st, w), cols[n]], lambda n, st, w: v_ref[pl.ds(st, w), cols[n]],
        [km_ref[:, c] for c in cols], [vm_ref[:, c] for c in cols], *scratch)
    lam4 = lam_ref[...]
    lam = (jnp.exp(jnp.sum(lam4[0:1] * lam4[1:2], axis=-1, keepdims=True))
           - jnp.exp(jnp.sum(lam4[2:3] * lam4[3:4], axis=-1, keepdims=True)) + lam_init)
    for g in range(len(qs) // 2):
        o = outs[2 * g] - lam * outs[2 * g + 1]
        o_ref[:, g * LANES:(g + 1) * LANES] = (_rms(o, g_ref[...]) * (1.0 - lam_init)).astype(o_ref.dtype)


def _attn_call(kernel, name, q, qw, qc, k, kw, kc, v, vc, extra, scratch, batch, seq, groups):
    rows = q.shape[0]
    t = ATT_TILE
    n_tok_tiles = seq // t
    tok_rows = batch * seq
    meta_q = tok_rows // t
    meta_k = tok_rows // N_META
    vw = WIDTH // groups

    def q_map(b, g, s):
        return (jnp.where(s > 0, b * n_tok_tiles + s - 1, meta_q), qc + g)

    def o_map(b, g, s):
        return (jnp.where(s > 0, b * n_tok_tiles + s - 1, meta_q + b), g)

    in_specs = [
        pl.BlockSpec((t, qw), q_map),
        pl.BlockSpec((seq, kw), lambda b, g, i: (b, kc + g)),
        pl.BlockSpec((seq, vw), lambda b, g, i: (b, vc + g)),
        pl.BlockSpec((N_META, kw), lambda b, g, i: (meta_k, kc + g)),
        pl.BlockSpec((N_META, vw), lambda b, g, i: (meta_k, vc + g)),
    ] + [pl.BlockSpec(a.shape, lambda b, g, i: (0, 0)) for a in extra]
    return pl.pallas_call(
        kernel,
        grid=(batch, groups, n_tok_tiles + 1),
        in_specs=in_specs,
        out_specs=pl.BlockSpec((t, vw), o_map),
        out_shape=jax.ShapeDtypeStruct((rows + (batch - 1) * t, WIDTH), BF16),
        scratch_shapes=scratch,
        compiler_params=_params(("parallel", "parallel", "arbitrary")),
        name=name,
    )(q, k, v, k, v, *extra)


def _perm_w_in(w_in):
    sizes = [512, 512, 512, 512, 384, 256, 32, 512, 512, 512, 512, 512, 3072]
    offs = [0]
    for s in sizes:
        offs.append(offs[-1] + s)
    seg = [w_in[..., offs[n]:offs[n + 1]] for n in range(len(sizes))]
    sb_q, sb_k, sb_v, sb_z, cq, ckv, kr, mz, dq, dk, dv, dz, gate = seg
    pad = lambda n: jnp.zeros(w_in.shape[:-1] + (n,), w_in.dtype)
    cols = [gate, sb_q, sb_k, sb_v, sb_z, dq, dk, dv, dz, mz, ckv, cq, pad(HEAD), kr, pad(LANES - HEAD - MLA_ROPE)]
    return jnp.concatenate(cols, axis=-1).astype(BF16)


def _rope_tables(seq):
    pos = np.concatenate([np.arange(seq) + N_META, np.arange(META_ROWS)]).astype(np.float64)[:, None]

    def tables(dim, starts):
        inv = ROPE_THETA ** (-np.arange(0, dim, 2, dtype=np.float64) / dim)
        ang = pos * inv[None, :]
        cos, sin = np.cos(ang), np.sin(ang)
        half = dim // 2
        c = np.ones((pos.shape[0], LANES))
        s1 = np.zeros((pos.shape[0], LANES))
        s2 = np.zeros((pos.shape[0], LANES))
        for st in starts:
            c[:, st:st + half] = cos
            c[:, st + half:st + dim] = cos
            s1[:, st:st + half] = -sin
            s2[:, st + half:st + dim] = sin
        return [jnp.asarray(a, F32) for a in (c, s1, s2)]

    return tables(MLA_ROPE, [HEAD]) + tables(DIFF_ROT, [0, HEAD])


def _pick(rows, candidates):
    for c in candidates:
        if rows % c == 0:
            return c
    raise ValueError(f"no row tile for {rows}")


def kernel(x, meta_tokens, norm_g, w_in, b_gate, mla_cq_g, mla_ckv_g, mla_w_uq, mla_w_ukv, diff_lambda,
           diff_norm_g, w_o_sb, w_o_mla, w_o_diff, w_out, final_g):
    batch, seq, d = x.shape
    depth = norm_g.shape[0]
    assert d == D_MODEL and seq % ATT_TILE == 0 and meta_tokens.shape == (N_META, D_MODEL)
    tok_rows = batch * seq
    rows = tok_rows + META_ROWS
    tm = ROW_TILE
    tm_proj = _pick(rows, (3328, 1280, 1024, 768, 512, 256))

    n_tok_tiles = tok_rows // tm
    h_tok = x.reshape(tok_rows, d)
    h_meta = jnp.concatenate([meta_tokens.astype(x.dtype), jnp.zeros((META_ROWS - N_META, d), x.dtype)], axis=0)
    meta_block = 0
    wuq = jnp.pad(mla_w_uq.reshape(depth, MLA_Q_RANK, MLA_HEADS, HEAD + MLA_ROPE),
                  ((0, 0), (0, 0), (0, 0), (0, LANES - HEAD - MLA_ROPE))).reshape(depth, MLA_Q_RANK, -1).astype(BF16)
    wukv = mla_w_ukv.reshape(depth, MLA_KV_RANK, MLA_HEADS, 2 * HEAD)
    wuk = jnp.pad(wukv[..., :HEAD], ((0, 0), (0, 0), (0, 0), (0, LANES - HEAD))).reshape(depth, MLA_KV_RANK, -1).astype(BF16)
    wuv = wukv[..., HEAD:].reshape(depth, MLA_KV_RANK, -1).astype(BF16)
    tabs = _rope_tables(seq)
    row2 = lambda a: a.reshape(1, -1)

    hn = _norm_call(h_tok, h_meta, row2(norm_g[0]), tm)
    out = None
    for l in range(depth):
        last = l == depth - 1
        proj = _proj_call(hn, _perm_w_in(w_in[l]), tm_proj)
        q_m, k_m, v_m, q_d, k_d = _prep_call(proj, row2(mla_cq_g[l]), row2(mla_ckv_g[l]), wuq[l], wuk[l], wuv[l],
                                             tabs, tm, tok_rows, seq)
        gw = ATT_GROUPS * LANES
        n_s = 2 * ATT_GROUPS
        o_sb = _attn_call(_sb_kernel, "sb_attn", proj, gw, C_SBQ // gw, proj, gw, C_SBK // gw,
                          proj, C_SBV // gw, [], _sb_scratch(n_s), batch, seq, WIDTH // gw)
        o_mla = _attn_call(_mla_kernel, "mla_attn", q_m, 2 * gw, 0, k_m, 2 * gw, 0, v_m, 0, [],
                           _softmax_scratch(n_s), batch, seq, WIDTH // gw)
        lam_init = 0.8 - 0.6 * math.exp(-0.3 * l)
        o_diff = _attn_call(functools.partial(_diff_kernel, lam_init=lam_init), "diff_attn",
                            q_d, gw, 0, k_d, gw, 0, proj, C_DV // gw,
                            [diff_lambda[l].astype(F32), row2(diff_norm_g[l])], _softmax_scratch(n_s), batch, seq,
                            WIDTH // gw)
        g_next = row2(final_g if last else norm_g[l + 1])
        res = _merge_call(o_sb, o_mla, o_diff, proj, h_tok, h_meta, meta_block, n_tok_tiles, row2(b_gate[l]),
                          w_o_sb[l].astype(BF16), w_o_mla[l].astype(BF16), w_o_diff[l].astype(BF16),
                          w_out[l].astype(BF16), g_next, tm, tok_rows if last else rows, last)
        if last:
            out = res[0]
        else:
            h, hn = res
            h_tok, h_meta, meta_block = h, h, n_tok_tiles
    return out.reshape(batch, seq, d)
```

```python
import functools
import math

import jax
import jax.numpy as jnp
import numpy as np
from jax import lax
from jax.experimental import pallas as pl
from jax.experimental.pallas import tpu as pltpu

F32 = jnp.float32
BF16 = jnp.bfloat16

D_MODEL = 1024
CHUNK = 64
N_META = 16
ROPE_THETA = 500000.0
EPS = 1e-6
LANES = 128
HEAD = 64
WIDTH = 512
MLA_HEADS = 8
MLA_ROPE = 32
MLA_Q_RANK = 384
MLA_KV_RANK = 256
DIFF_ROT = 16

META_ROWS = 256
ROW_TILE = 256
ATT_TILE = 256
ROW_CHUNK = 128
SCORE_LOOKAHEAD = 8
ATT_GROUPS = 4
NEG_BIG = -1e30
LOG2E = 1.4426950408889634
SB_DEAD_RUN = -104.0
VMEM_LIMIT = 56 * 1024 * 1024

C_GATE = 0
C_SBQ = 3072
C_SBK = 3584
C_SBV = 4096
C_SBZ = 4608
C_DQ = 5120
C_DK = 5632
C_DV = 6144
C_DZ = 6656
C_MZ = 7168
C_LAT = 7680
LAT_W = 768
N_PROJ = 8448
PROJ_TN = 768


def _params(sem):
    return pltpu.CompilerParams(dimension_semantics=sem, vmem_limit_bytes=VMEM_LIMIT)


def _rms(x32, g):
    ms = jnp.mean(x32 * x32, axis=-1, keepdims=True)
    return x32 * lax.rsqrt(ms + EPS) * g


def _residual_rows(tok_ref, meta_ref, n_tok_tiles):
    return jnp.where(pl.program_id(0) < n_tok_tiles, tok_ref[...], meta_ref[...])


def _residual_specs(tm, n_tok_tiles, meta_block):
    return [pl.BlockSpec((tm, D_MODEL), lambda i: (jnp.minimum(i, n_tok_tiles - 1), 0)),
            pl.BlockSpec((tm, D_MODEL), lambda i: (meta_block, 0))]


def _norm_kernel(tok_ref, meta_ref, g_ref, o_ref, *, n_tok_tiles):
    o_ref[...] = _rms(_residual_rows(tok_ref, meta_ref, n_tok_tiles), g_ref[...]).astype(o_ref.dtype)


def _norm_call(h_tok, h_meta, g, tm):
    n_tok_tiles = h_tok.shape[0] // tm
    rows = h_tok.shape[0] + h_meta.shape[0]
    return pl.pallas_call(
        functools.partial(_norm_kernel, n_tok_tiles=n_tok_tiles),
        grid=(rows // tm,),
        in_specs=_residual_specs(tm, n_tok_tiles, 0) + [pl.BlockSpec((1, D_MODEL), lambda i: (0, 0))],
        out_specs=pl.BlockSpec((tm, D_MODEL), lambda i: (i, 0)),
        out_shape=jax.ShapeDtypeStruct((rows, D_MODEL), BF16),
        compiler_params=_params(("parallel",)),
        name="norm_in",
    )(h_tok, h_meta, g)


def _proj_kernel(a_ref, w_ref, o_ref):
    o_ref[...] = jnp.dot(a_ref[...], w_ref[...], preferred_element_type=F32).astype(o_ref.dtype)


def _proj_call(hn, w, tm):
    rows = hn.shape[0]
    return pl.pallas_call(
        _proj_kernel,
        grid=(rows // tm, N_PROJ // PROJ_TN),
        in_specs=[pl.BlockSpec((tm, D_MODEL), lambda i, j: (i, 0)),
                  pl.BlockSpec((D_MODEL, PROJ_TN), lambda i, j: (0, j))],
        out_specs=pl.BlockSpec((tm, PROJ_TN), lambda i, j: (i, j)),
        out_shape=jax.ShapeDtypeStruct((rows, N_PROJ), BF16),
        compiler_params=_params(("parallel", "arbitrary")),
        name="in_proj",
    )(hn, w)


def _rot(x, c, s1, s2, shift):
    return x * c + pltpu.roll(x, LANES - shift, 1) * s1 + pltpu.roll(x, shift, 1) * s2


def _prep_kernel(lat_ref, dq_ref, dk_ref, gq_ref, gkv_ref, wuq_ref, wuk_ref, wuv_ref,
                 cm_ref, s1m_ref, s2m_ref, cd_ref, s1d_ref, s2d_ref,
                 qm_ref, km_ref, vm_ref, dqo_ref, dko_ref, *, n_tok_tiles, per_seq):
    tm = lat_ref.shape[0]
    i = pl.program_id(0)
    tab_rows = pl.ds(pl.multiple_of(jnp.where(i < n_tok_tiles, i % per_seq, per_seq) * tm, tm), tm)
    lat = lat_ref[...].astype(F32)
    ckv = lat[:, :MLA_KV_RANK]
    cq = lat[:, MLA_KV_RANK:MLA_KV_RANK + MLA_Q_RANK]
    kr = lat[:, MLA_KV_RANK + MLA_Q_RANK:]
    ncq = _rms(cq, gq_ref[...]).astype(BF16)
    nckv = _rms(ckv, gkv_ref[...]).astype(BF16)
    cm, s1m, s2m = cm_ref[tab_rows, :], s1m_ref[tab_rows, :], s2m_ref[tab_rows, :]
    scale_b = LOG2E / math.sqrt(HEAD + MLA_ROPE)
    k_rope = _rot(kr, cm, s1m, s2m, MLA_ROPE // 2)
    vm_ref[...] = jnp.dot(nckv, wuv_ref[...], preferred_element_type=F32).astype(BF16)
    for h in range(MLA_HEADS):
        sl = slice(h * LANES, (h + 1) * LANES)
        qf = jnp.dot(ncq, wuq_ref[:, sl], preferred_element_type=F32)
        qm_ref[:, sl] = (_rot(qf, cm, s1m, s2m, MLA_ROPE // 2) * scale_b).astype(BF16)
        kf = jnp.dot(nckv, wuk_ref[:, sl], preferred_element_type=F32)
        km_ref[:, sl] = (kf + k_rope).astype(BF16)
    cd, s1d, s2d = cd_ref[tab_rows, :], s1d_ref[tab_rows, :], s2d_ref[tab_rows, :]
    for h in range(WIDTH // LANES):
        sl = slice(h * LANES, (h + 1) * LANES)
        dqo_ref[:, sl] = (_rot(dq_ref[:, sl].astype(F32), cd, s1d, s2d, DIFF_ROT // 2) * (LOG2E / math.sqrt(HEAD))).astype(BF16)
        dko_ref[:, sl] = _rot(dk_ref[:, sl].astype(F32), cd, s1d, s2d, DIFF_ROT // 2).astype(BF16)


def _prep_call(proj, gq, gkv, wuq, wuk, wuv, tabs, tm, tok_rows, seq):
    rows = proj.shape[0]
    n_tok_tiles = tok_rows // tm
    per_seq = seq // tm

    row = lambda w, c: pl.BlockSpec((tm, w), lambda i: (i, c))
    full = lambda a: pl.BlockSpec(a.shape, lambda i: (0, 0))
    out = lambda w: jax.ShapeDtypeStruct((rows, w), BF16)
    return pl.pallas_call(
        functools.partial(_prep_kernel, n_tok_tiles=n_tok_tiles, per_seq=per_seq),
        grid=(rows // tm,),
        in_specs=[row(LAT_W, C_LAT // LAT_W), row(WIDTH, C_DQ // WIDTH), row(WIDTH, C_DK // WIDTH),
                  full(gq), full(gkv), full(wuq), full(wuk), full(wuv)] + [full(tab) for tab in tabs],
        out_specs=[row(MLA_HEADS * LANES, 0), row(MLA_HEADS * LANES, 0), row(WIDTH, 0),
                   row(WIDTH, 0), row(WIDTH, 0)],
        out_shape=[out(MLA_HEADS * LANES), out(MLA_HEADS * LANES), out(WIDTH), out(WIDTH), out(WIDTH)],
        compiler_params=_params(("parallel",)),
        name="prep",
    )(proj, proj, proj, gq, gkv, wuq, wuk, wuv, *tabs)


def _sigmoid(x):
    return 0.5 * jnp.tanh(0.5 * x) + 0.5


def _merge_kernel(osb_ref, omla_ref, odiff_ref, zsb_ref, zmla_ref, zdiff_ref, g0_ref, g1_ref, g2_ref,
                  htok_ref, hmeta_ref, bg_ref, wsb_ref, wmla_ref, wdiff_ref, wout_ref, gn_ref, *out_refs,
                  last, n_tok_tiles):
    def branch(o_ref, z_ref, w_ref):
        z = z_ref[...].astype(F32)
        a = o_ref[...].astype(F32) * (z * _sigmoid(z))
        return jnp.dot(a.astype(BF16), w_ref[...], preferred_element_type=F32)

    bg = bg_ref[...]

    def twice_gate(g_ref, n):
        return 1.0 + jnp.tanh(0.5 * (g_ref[...].astype(F32) + bg[:, n * D_MODEL:(n + 1) * D_MODEL]))

    merged = twice_gate(g0_ref, 0) * branch(osb_ref, zsb_ref, wsb_ref)
    merged += twice_gate(g1_ref, 1) * branch(omla_ref, zmla_ref, wmla_ref)
    merged += twice_gate(g2_ref, 2) * branch(odiff_ref, zdiff_ref, wdiff_ref)
    merged *= 0.5
    h_old = _residual_rows(htok_ref, hmeta_ref, n_tok_tiles)
    h_new = h_old + jnp.dot(merged.astype(BF16), wout_ref[...], preferred_element_type=F32)
    normed = _rms(h_new, gn_ref[...])
    if last:
        out_refs[0][...] = normed
    else:
        out_refs[0][...] = h_new
        out_refs[1][...] = normed.astype(BF16)


def _merge_call(o_sb, o_mla, o_diff, proj, h_tok, h_meta, meta_block, n_tok_tiles, bg, w_sb, w_mla, w_diff, w_out,
                g_next, tm, out_rows, last):
    row = lambda w, c: pl.BlockSpec((tm, w), lambda i: (i, c))
    full = lambda a: pl.BlockSpec(a.shape, lambda i: (0, 0))
    if last:
        out_specs = [row(D_MODEL, 0)]
        out_shape = [jax.ShapeDtypeStruct((out_rows, D_MODEL), F32)]
    else:
        out_specs = [row(D_MODEL, 0), row(D_MODEL, 0)]
        out_shape = [jax.ShapeDtypeStruct((out_rows, D_MODEL), F32),
                     jax.ShapeDtypeStruct((out_rows, D_MODEL), BF16)]
    return pl.pallas_call(
        functools.partial(_merge_kernel, last=last, n_tok_tiles=n_tok_tiles),
        grid=(out_rows // tm,),
        in_specs=[row(WIDTH, 0), row(WIDTH, 0), row(WIDTH, 0),
                  row(WIDTH, C_SBZ // WIDTH), row(WIDTH, C_MZ // WIDTH), row(WIDTH, C_DZ // WIDTH),
                  row(D_MODEL, 0), row(D_MODEL, 1), row(D_MODEL, 2)]
                 + _residual_specs(tm, n_tok_tiles, meta_block)
                 + [full(bg), full(w_sb), full(w_mla), full(w_diff), full(w_out), full(g_next)],
        out_specs=out_specs,
        out_shape=out_shape,
        compiler_params=_params(("parallel",)),
        name="merge_last" if last else "merge",
    )(o_sb, o_mla, o_diff, proj, proj, proj, proj, proj, proj, h_tok, h_meta, bg, w_sb, w_mla, w_diff, w_out, g_next)


def _qk(q, k):
    return lax.dot_general(q, k, (((1,), (1,)), ((), ())), preferred_element_type=F32)


def _iota2(shape, axis):
    return lax.broadcasted_iota(jnp.int32, shape, axis)


def _half_mask(rows, half):
    lane = _iota2((rows, LANES), 1)
    return (lane >= HEAD) if half else (lane < HEAD)


def _pv(p, v):
    return jnp.dot(p.astype(BF16), v, preferred_element_type=F32)


def _incl_tri(n):
    return (_iota2((n, n), 0) >= _iota2((n, n), 1)).astype(BF16)


def _sb_logw(z, tri):
    nz = -z
    lk = jnp.minimum(nz, 0.0) - jnp.log(1.0 + jnp.exp(jnp.minimum(z, nz)))
    cum = jnp.dot(lk.astype(BF16), tri, preferred_element_type=F32)
    return z + cum, cum[:, 0:1]


def _wide(x, width):
    return jnp.concatenate([x] * (width // LANES), axis=-1)


def _row_blocks(t):
    return [pl.ds(r * ROW_CHUNK, ROW_CHUNK) for r in range(t // ROW_CHUNK)]


def _stream_cols(n_streams):
    return [slice((n // 2) * LANES, (n // 2 + 1) * LANES) for n in range(n_streams)]


def _half_queries(q_ref, t, scale):
    qs = []
    for n, c in enumerate(_stream_cols(2 * (q_ref.shape[1] // LANES))):
        qg = q_ref[:, c] if scale is None else q_ref[:, c] * scale
        qs.append(jnp.where(_half_mask(t, n % 2), qg, jnp.zeros_like(qg)))
    return qs


def _join_halves(o_ref, outs, t):
    for g in range(len(outs) // 2):
        o_ref[:, g * LANES:(g + 1) * LANES] = jnp.where(_half_mask(t, 0), outs[2 * g], outs[2 * g + 1]).astype(o_ref.dtype)


def _sb_kernel(q_ref, k_ref, v_ref, km_ref, vm_ref, o_ref, acc_scr, run_scr):
    i = pl.program_id(2) - 1
    t = ATT_TILE
    qs = _half_queries(q_ref, t, 0.125)
    n_s = len(qs)
    cols = _stream_cols(n_s)
    tri = _incl_tri(t)
    tri_m = _incl_tri(N_META)
    kms = [km_ref[:, c] for c in cols]
    vms = [vm_ref[:, c] for c in cols]
    blocks = _row_blocks(t)

    def key_step(kbs, vbs, tri_k, masks):
        def scores(n):
            return _qk(qs[n], kbs[n])

        def log_weights(n, z):
            parts = []
            for b in range(len(blocks)):
                zb = z[b * ROW_CHUNK:(b + 1) * ROW_CHUNK]
                tri_b = tri_k
                if masks is not None:
                    live = (b + 1) * ROW_CHUNK
                    zb = jnp.where(masks[b][:, :live], zb[:, :live], NEG_BIG)
                    tri_b = tri_k[:live, :live]
                parts.append(_sb_logw(zb, tri_b))
            return parts

        def fold(n, parts):
            runs = []
            for r, (part, total) in zip(blocks, parts):
                run = run_scr[n, r, :]
                width = part.shape[1]
                w = jnp.exp(part + (_wide(run, width) if width > LANES else run[:, :width]))
                acc_scr[n, r, :] += _pv(w, vbs[n][:width])
                run_scr[n, r, :] = run + total
                runs.append(run + total)
            return runs

        z = {0: scores(0)}
        if n_s > 1:
            z[1] = scores(1)
        lw = {0: log_weights(0, z.pop(0))}
        runs = []
        for n in range(n_s):
            if n + 2 < n_s:
                z[n + 2] = scores(n + 2)
            if n + 1 < n_s:
                lw[n + 1] = log_weights(n + 1, z.pop(n + 1))
            runs += fold(n, lw.pop(n))
        return jnp.max(functools.reduce(jnp.maximum, runs))

    def tiles_at(ref, p):
        rows = pl.ds(pl.multiple_of((i - p) * t, t), t)
        per_group = [ref[rows, cols[2 * g]] for g in range(n_s // 2)]
        return [per_group[n // 2] for n in range(n_s)]

    def tok():
        causal = [_iota2((ROW_CHUNK, t), 1) < _iota2((ROW_CHUNK, t), 0) + b * ROW_CHUNK for b in range(len(blocks))]
        acc_scr[...] = jnp.zeros_like(acc_scr)
        run_scr[...] = jnp.zeros_like(run_scr)
        first = key_step(tiles_at(k_ref, 0), tiles_at(v_ref, 0), tri, causal)

        def alive(c):
            return jnp.logical_and(c[0] <= i, c[1] >= SB_DEAD_RUN)

        def older(c):
            p = c[0]
            return p + 1, key_step(tiles_at(k_ref, p), tiles_at(v_ref, p), tri, None)

        _, largest = lax.while_loop(alive, older, (jnp.int32(1), first))

        @pl.when(largest >= SB_DEAD_RUN)
        def _():
            key_step(kms, vms, tri_m, None)

        return [acc_scr[n] for n in range(n_s)]

    def meta():
        causal_m = _iota2((t, N_META), 1) < _iota2((t, N_META), 0)
        zm = [_qk(qh, km) for qh, km in zip(qs, kms)]
        return [_pv(jnp.exp(_sb_logw(jnp.where(causal_m, z, NEG_BIG), tri_m)[0]), vm) for z, vm in zip(zm, vms)]

    _join_halves(o_ref, lax.cond(i >= 0, tok, meta), t)


def _sb_scratch(n_s):
    t = ATT_TILE
    return [pltpu.VMEM((n_s, t, LANES), F32), pltpu.VMEM((n_s, t, LANES), F32)]


def _row_max(s):
    return jnp.max(s, axis=-1, keepdims=True)


def _row_sum(p):
    return jnp.sum(p, axis=-1, keepdims=True)


def _softmax_attend(i, qs, k_tile, v_tile, kms, vms, acc_scr):
    t = ATT_TILE
    n_s = len(qs)

    def scores_t(keys, q):
        return lax.dot_general(keys, q, (((1,), (1,)), ((), ())), preferred_element_type=F32)

    def pv_t(v, p_t):
        return lax.dot_general(v, p_t.astype(BF16), (((0,), (0,)), ((), ())), preferred_element_type=F32)

    def col_max(s):
        return jnp.max(s, axis=0, keepdims=True)

    def col_sum(p):
        return jnp.sum(p, axis=0, keepdims=True)

    streams = range(n_s)

    def sweep(start, width, stats):
        ss = {n: scores_t(k_tile(n, start, width), qs[n]) for n in range(min(SCORE_LOOKAHEAD, n_s))}
        out, updates = [], []
        for n in streams:
            s = ss.pop(n)
            m_new = jnp.maximum(stats[2 * n], col_max(s))
            alpha = jnp.exp2(stats[2 * n] - m_new)
            p = jnp.exp2(s - m_new)
            out += [m_new, alpha * stats[2 * n + 1] + col_sum(p)]
            updates.append((alpha, pv_t(v_tile(n, start, width), p)))
            if n + SCORE_LOOKAHEAD < n_s:
                ss[n + SCORE_LOOKAHEAD] = scores_t(k_tile(n + SCORE_LOOKAHEAD, start, width), qs[n + SCORE_LOOKAHEAD])
        for n, (alpha, pv) in enumerate(updates):
            acc_scr[n] = alpha * acc_scr[n] + pv
        return tuple(out)

    def tok():
        def first(before):
            start = pl.multiple_of((i - before) * t, (1 + before) * t)
            width = (1 + before) * t
            chunk_ok = _iota2((width, t), 0) // CHUNK <= _iota2((width, t), 1) // CHUNK + before * (t // CHUNK)
            sds = [jnp.where(chunk_ok, scores_t(k_tile(n, start, width), qs[n]), NEG_BIG) for n in streams]
            sms = [scores_t(kms[n], qs[n]) for n in streams]
            ms = [jnp.maximum(col_max(sds[n]), col_max(sms[n])) for n in streams]
            pds = [jnp.exp2(sds[n] - ms[n]) for n in streams]
            pms = [jnp.exp2(sms[n] - ms[n]) for n in streams]
            ls = [col_sum(pds[n]) + col_sum(pms[n]) for n in streams]
            for n in streams:
                acc_scr[n] = pv_t(v_tile(n, start, width), pds[n]) + pv_t(vms[n], pms[n])
            return tuple(x for n in streams for x in (ms[n], ls[n]))

        stats = lax.cond(i % 2 == 1, lambda: first(1), lambda: first(0))
        stats = lax.fori_loop(0, i // 2, lambda j, st: sweep(pl.multiple_of(j * 2 * t, 2 * t), 2 * t, st), stats)
        return [(acc_scr[n] * (1.0 / stats[2 * n + 1])).T for n in range(n_s)]

    def meta():
        sms = [_qk(q, km) for q, km in zip(qs, kms)]
        pms = [jnp.exp2(sm - _row_max(sm)) for sm in sms]
        return [_pv(pm, vm) * (1.0 / _row_sum(pm)) for pm, vm in zip(pms, vms)]

    return lax.cond(i >= 0, tok, meta)


def _softmax_scratch(n_s):
    t = ATT_TILE
    return [pltpu.VMEM((n_s, LANES, t), F32)]


def _mla_kernel(q_ref, k_ref, v_ref, km_ref, vm_ref, o_ref, *scratch):
    i = pl.program_id(2) - 1
    n_s = q_ref.shape[1] // LANES
    qcols = [slice(n * LANES, (n + 1) * LANES) for n in range(n_s)]
    vcols = _stream_cols(n_s)
    outs = _softmax_attend(
        i, [q_ref[:, c] for c in qcols],
        lambda n, st, w: k_ref[pl.ds(st, w), qcols[n]], lambda n, st, w: v_ref[pl.ds(st, w), vcols[n]],
        [km_ref[:, c] for c in qcols], [vm_ref[:, c] for c in vcols], *scratch)
    _join_halves(o_ref, outs, ATT_TILE)


def _diff_kernel(q_ref, k_ref, v_ref, km_ref, vm_ref, lam_ref, g_ref, o_ref, *scratch, lam_init):
    i = pl.program_id(2) - 1
    qs = _half_queries(q_ref, ATT_TILE, None)
    cols = _stream_cols(len(qs))
    outs = _softmax_attend(
        i, qs,
        lambda n, st, w: k_ref[pl.ds(st, w), cols[n]], lambda n, st, w: v_ref[pl.ds(st, w), cols[n]],
        [km_ref[:, c] for c in cols], [vm_ref[:, c] for c in cols], *scratch)
    lam4 = lam_ref[...]
    lam = (jnp.exp(jnp.sum(lam4[0:1] * lam4[1:2], axis=-1, keepdims=True))
           - jnp.exp(jnp.sum(lam4[2:3] * lam4[3:4], axis=-1, keepdims=True)) + lam_init)
    for g in range(len(qs) // 2):
        o = outs[2 * g] - lam * outs[2 * g + 1]
        o_ref[:, g * LANES:(g + 1) * LANES] = (_rms(o, g_ref[...]) * (1.0 - lam_init)).astype(o_ref.dtype)


def _attn_call(kernel, name, q, qw, qc, k, kw, kc, v, vc, extra, scratch, batch, seq, groups):
    rows = q.shape[0]
    t = ATT_TILE
    n_tok_tiles = seq // t
    tok_rows = batch * seq
    meta_q = tok_rows // t
    meta_k = tok_rows // N_META
    vw = WIDTH // groups

    def q_map(b, g, s):
        return (jnp.where(s > 0, b * n_tok_tiles + s - 1, meta_q), qc + g)

    def o_map(b, g, s):
        return (jnp.where(s > 0, b * n_tok_tiles + s - 1, meta_q + b), g)

    in_specs = [
        pl.BlockSpec((t, qw), q_map),
        pl.BlockSpec((seq, kw), lambda b, g, i: (b, kc + g)),
        pl.BlockSpec((seq, vw), lambda b, g, i: (b, vc + g)),
        pl.BlockSpec((N_META, kw), lambda b, g, i: (meta_k, kc + g)),
        pl.BlockSpec((N_META, vw), lambda b, g, i: (meta_k, vc + g)),
    ] + [pl.BlockSpec(a.shape, lambda b, g, i: (0, 0)) for a in extra]
    return pl.pallas_call(
        kernel,
        grid=(batch, groups, n_tok_tiles + 1),
        in_specs=in_specs,
        out_specs=pl.BlockSpec((t, vw), o_map),
        out_shape=jax.ShapeDtypeStruct((rows + (batch - 1) * t, WIDTH), BF16),
        scratch_shapes=scratch,
        compiler_params=_params(("parallel", "parallel", "arbitrary")),
        name=name,
    )(q, k, v, k, v, *extra)


def _perm_w_in(w_in):
    sizes = [512, 512, 512, 512, 384, 256, 32, 512, 512, 512, 512, 512, 3072]
    offs = [0]
    for s in sizes:
        offs.append(offs[-1] + s)
    seg = [w_in[..., offs[n]:offs[n + 1]] for n in range(len(sizes))]
    sb_q, sb_k, sb_v, sb_z, cq, ckv, kr, mz, dq, dk, dv, dz, gate = seg
    pad = lambda n: jnp.zeros(w_in.shape[:-1] + (n,), w_in.dtype)
    cols = [gate, sb_q, sb_k, sb_v, sb_z, dq, dk, dv, dz, mz, ckv, cq, pad(HEAD), kr, pad(LANES - HEAD - MLA_ROPE)]
    return jnp.concatenate(cols, axis=-1).astype(BF16)


def _rope_tables(seq):
    pos = np.concatenate([np.arange(seq) + N_META, np.arange(META_ROWS)]).astype(np.float64)[:, None]

    def tables(dim, starts):
        inv = ROPE_THETA ** (-np.arange(0, dim, 2, dtype=np.float64) / dim)
        ang = pos * inv[None, :]
        cos, sin = np.cos(ang), np.sin(ang)
        half = dim // 2
        c = np.ones((pos.shape[0], LANES))
        s1 = np.zeros((pos.shape[0], LANES))
        s2 = np.zeros((pos.shape[0], LANES))
        for st in starts:
            c[:, st:st + half] = cos
            c[:, st + half:st + dim] = cos
            s1[:, st:st + half] = -sin
            s2[:, st + half:st + dim] = sin
        return [jnp.asarray(a, F32) for a in (c, s1, s2)]

    return tables(MLA_ROPE, [HEAD]) + tables(DIFF_ROT, [0, HEAD])


def _pick(rows, candidates):
    for c in candidates:
        if rows % c == 0:
            return c
    raise ValueError(f"no row tile for {rows}")


def kernel(x, meta_tokens, norm_g, w_in, b_gate, mla_cq_g, mla_ckv_g, mla_w_uq, mla_w_ukv, diff_lambda,
           diff_norm_g, w_o_sb, w_o_mla, w_o_diff, w_out, final_g):
    batch, seq, d = x.shape
    depth = norm_g.shape[0]
    assert d == D_MODEL and seq % ATT_TILE == 0 and meta_tokens.shape == (N_META, D_MODEL)
    tok_rows = batch * seq
    rows = tok_rows + META_ROWS
    tm = ROW_TILE
    tm_proj = _pick(rows, (3328, 1280, 1024, 768, 512, 256))

    n_tok_tiles = tok_rows // tm
    h_tok = x.reshape(tok_rows, d)
    h_meta = jnp.concatenate([meta_tokens.astype(x.dtype), jnp.zeros((META_ROWS - N_META, d), x.dtype)], axis=0)
    meta_block = 0
    wuq = jnp.pad(mla_w_uq.reshape(depth, MLA_Q_RANK, MLA_HEADS, HEAD + MLA_ROPE),
                  ((0, 0), (0, 0), (0, 0), (0, LANES - HEAD - MLA_ROPE))).reshape(depth, MLA_Q_RANK, -1).astype(BF16)
    wukv = mla_w_ukv.reshape(depth, MLA_KV_RANK, MLA_HEADS, 2 * HEAD)
    wuk = jnp.pad(wukv[..., :HEAD], ((0, 0), (0, 0), (0, 0), (0, LANES - HEAD))).reshape(depth, MLA_KV_RANK, -1).astype(BF16)
    wuv = wukv[..., HEAD:].reshape(depth, MLA_KV_RANK, -1).astype(BF16)
    tabs = _rope_tables(seq)
    row2 = lambda a: a.reshape(1, -1)

    hn = _norm_call(h_tok, h_meta, row2(norm_g[0]), tm)
    out = None
    for l in range(depth):
        last = l == depth - 1
        proj = _proj_call(hn, _perm_w_in(w_in[l]), tm_proj)
        q_m, k_m, v_m, q_d, k_d = _prep_call(proj, row2(mla_cq_g[l]), row2(mla_ckv_g[l]), wuq[l], wuk[l], wuv[l],
                                             tabs, tm, tok_rows, seq)
        gw = ATT_GROUPS * LANES
        n_s = 2 * ATT_GROUPS
        o_sb = _attn_call(_sb_kernel, "sb_attn", proj, gw, C_SBQ // gw, proj, gw, C_SBK // gw,
                          proj, C_SBV // gw, [], _sb_scratch(n_s), batch, seq, WIDTH // gw)
        o_mla = _attn_call(_mla_kernel, "mla_attn", q_m, 2 * gw, 0, k_m, 2 * gw, 0, v_m, 0, [],
                           _softmax_scratch(n_s), batch, seq, WIDTH // gw)
        lam_init = 0.8 - 0.6 * math.exp(-0.3 * l)
        o_diff = _attn_call(functools.partial(_diff_kernel, lam_init=lam_init), "diff_attn",
                            q_d, gw, 0, k_d, gw, 0, proj, C_DV // gw,
                            [diff_lambda[l].astype(F32), row2(diff_norm_g[l])], _softmax_scratch(n_s), batch, seq,
                            WIDTH // gw)
        g_next = row2(final_g if last else norm_g[l + 1])
        res = _merge_call(o_sb, o_mla, o_diff, proj, h_tok, h_meta, meta_block, n_tok_tiles, row2(b_gate[l]),
                          w_o_sb[l].astype(BF16), w_o_mla[l].astype(BF16), w_o_diff[l].astype(BF16),
                          w_out[l].astype(BF16), g_next, tm, tok_rows if last else rows, last)
        if last:
            out = res[0]
        else:
            h, hn = res
            h_tok, h_meta, meta_block = h, h, n_tok_tiles
    return out.reshape(batch, seq, d)
```

```python
import functools
import math

import jax
import jax.numpy as jnp
import numpy as np
from jax import lax
from jax.experimental import pallas as pl
from jax.experimental.pallas import tpu as pltpu

F32 = jnp.float32
BF16 = jnp.bfloat16

D_MODEL = 1024
CHUNK = 64
N_META = 16
ROPE_THETA = 500000.0
EPS = 1e-6
LANES = 128
HEAD = 64
WIDTH = 512
MLA_HEADS = 8
MLA_ROPE = 32
MLA_Q_RANK = 384
MLA_KV_RANK = 256
DIFF_ROT = 16

META_ROWS = 256
ROW_TILE = 256
ATT_TILE = 256
ROW_CHUNK = 128
SCORE_LOOKAHEAD = 8
ATT_GROUPS = 4
NEG_BIG = -1e30
LOG2E = 1.4426950408889634
SB_DEAD_RUN = -104.0
VMEM_LIMIT = 56 * 1024 * 1024

C_GATE = 0
C_SBQ = 3072
C_SBK = 3584
C_SBV = 4096
C_SBZ = 4608
C_DQ = 5120
C_DK = 5632
C_DV = 6144
C_DZ = 6656
C_MZ = 7168
C_LAT = 7680
LAT_W = 768
N_PROJ = 8448
PROJ_TN = 768


def _params(sem):
    return pltpu.CompilerParams(dimension_semantics=sem, vmem_limit_bytes=VMEM_LIMIT)


def _rms(x32, g):
    ms = jnp.mean(x32 * x32, axis=-1, keepdims=True)
    return x32 * lax.rsqrt(ms + EPS) * g


def _residual_rows(tok_ref, meta_ref, n_tok_tiles):
    return jnp.where(pl.program_id(0) < n_tok_tiles, tok_ref[...], meta_ref[...])


def _residual_specs(tm, n_tok_tiles, meta_block, width=D_MODEL):
    return [pl.BlockSpec((tm, width), lambda i: (jnp.minimum(i, n_tok_tiles - 1), 0)),
            pl.BlockSpec((tm, width), lambda i: (meta_block, 0))]


def _norm_kernel(tok_ref, meta_ref, g_ref, o_ref, *, n_tok_tiles):
    o_ref[...] = _rms(_residual_rows(tok_ref, meta_ref, n_tok_tiles), g_ref[...]).astype(o_ref.dtype)


def _norm_call(h_tok, h_meta, g, tm):
    n_tok_tiles = h_tok.shape[0] // tm
    rows = h_tok.shape[0] + h_meta.shape[0]
    return pl.pallas_call(
        functools.partial(_norm_kernel, n_tok_tiles=n_tok_tiles),
        grid=(rows // tm,),
        in_specs=_residual_specs(tm, n_tok_tiles, 0) + [pl.BlockSpec((1, D_MODEL), lambda i: (0, 0))],
        out_specs=pl.BlockSpec((tm, D_MODEL), lambda i: (i, 0)),
        out_shape=jax.ShapeDtypeStruct((rows, D_MODEL), BF16),
        compiler_params=_params(("parallel",)),
        name="norm_in",
    )(h_tok, h_meta, g)


def _proj_kernel(a_ref, w_ref, o_ref):
    o_ref[...] = jnp.dot(a_ref[...], w_ref[...], preferred_element_type=F32).astype(o_ref.dtype)


def _proj_call(hn, w, tm):
    rows = hn.shape[0]
    return pl.pallas_call(
        _proj_kernel,
        grid=(rows // tm, N_PROJ // PROJ_TN),
        in_specs=[pl.BlockSpec((tm, D_MODEL), lambda i, j: (i, 0)),
                  pl.BlockSpec((D_MODEL, PROJ_TN), lambda i, j: (0, j))],
        out_specs=pl.BlockSpec((tm, PROJ_TN), lambda i, j: (i, j)),
        out_shape=jax.ShapeDtypeStruct((rows, N_PROJ), BF16),
        compiler_params=_params(("parallel", "arbitrary")),
        name="in_proj",
    )(hn, w)


def _rot(x, c, s1, s2, shift):
    return x * c + pltpu.roll(x, LANES - shift, 1) * s1 + pltpu.roll(x, shift, 1) * s2


def _prep_kernel(lat_ref, dq_ref, dk_ref, gq_ref, gkv_ref, wuq_ref, wuk_ref, wuv_ref,
                 cm_ref, s1m_ref, s2m_ref, cd_ref, s1d_ref, s2d_ref,
                 qm_ref, km_ref, vm_ref, dqo_ref, dko_ref, *, n_tok_tiles, per_seq):
    tm = lat_ref.shape[0]
    i = pl.program_id(0)
    tab_rows = pl.ds(pl.multiple_of(jnp.where(i < n_tok_tiles, i % per_seq, per_seq) * tm, tm), tm)
    lat = lat_ref[...].astype(F32)
    ckv = lat[:, :MLA_KV_RANK]
    cq = lat[:, MLA_KV_RANK:MLA_KV_RANK + MLA_Q_RANK]
    kr = lat[:, MLA_KV_RANK + MLA_Q_RANK:]
    ncq = _rms(cq, gq_ref[...]).astype(BF16)
    nckv = _rms(ckv, gkv_ref[...]).astype(BF16)
    cm, s1m, s2m = cm_ref[tab_rows, :], s1m_ref[tab_rows, :], s2m_ref[tab_rows, :]
    scale_b = LOG2E / math.sqrt(HEAD + MLA_ROPE)
    k_rope = _rot(kr, cm, s1m, s2m, MLA_ROPE // 2)
    vm_ref[...] = jnp.dot(nckv, wuv_ref[...], preferred_element_type=F32).astype(BF16)
    for h in range(MLA_HEADS):
        sl = slice(h * LANES, (h + 1) * LANES)
        qf = jnp.dot(ncq, wuq_ref[:, sl], preferred_element_type=F32)
        qm_ref[:, sl] = (_rot(qf, cm, s1m, s2m, MLA_ROPE // 2) * scale_b).astype(BF16)
        kf = jnp.dot(nckv, wuk_ref[:, sl], preferred_element_type=F32)
        km_ref[:, sl] = (kf + k_rope).astype(BF16)
    cd, s1d, s2d = cd_ref[tab_rows, :], s1d_ref[tab_rows, :], s2d_ref[tab_rows, :]
    for h in range(WIDTH // LANES):
        sl = slice(h * LANES, (h + 1) * LANES)
        dqo_ref[:, sl] = (_rot(dq_ref[:, sl].astype(F32), cd, s1d, s2d, DIFF_ROT // 2) * (LOG2E / math.sqrt(HEAD))).astype(BF16)
        dko_ref[:, sl] = _rot(dk_ref[:, sl].astype(F32), cd, s1d, s2d, DIFF_ROT // 2).astype(BF16)


def _prep_call(proj, gq, gkv, wuq, wuk, wuv, tabs, tm, tok_rows, seq):
    rows = proj.shape[0]
    n_tok_tiles = tok_rows // tm
    per_seq = seq // tm

    row = lambda w, c: pl.BlockSpec((tm, w), lambda i: (i, c))
    full = lambda a: pl.BlockSpec(a.shape, lambda i: (0, 0))
    out = lambda w: jax.ShapeDtypeStruct((rows, w), BF16)
    return pl.pallas_call(
        functools.partial(_prep_kernel, n_tok_tiles=n_tok_tiles, per_seq=per_seq),
        grid=(rows // tm,),
        in_specs=[row(LAT_W, C_LAT // LAT_W), row(WIDTH, C_DQ // WIDTH), row(WIDTH, C_DK // WIDTH),
                  full(gq), full(gkv), full(wuq), full(wuk), full(wuv)] + [full(tab) for tab in tabs],
        out_specs=[row(MLA_HEADS * LANES, 0), row(MLA_HEADS * LANES, 0), row(WIDTH, 0),
                   row(WIDTH, 0), row(WIDTH, 0)],
        out_shape=[out(MLA_HEADS * LANES), out(MLA_HEADS * LANES), out(WIDTH), out(WIDTH), out(WIDTH)],
        compiler_params=_params(("parallel",)),
        name="prep",
    )(proj, proj, proj, gq, gkv, wuq, wuk, wuv, *tabs)


def _sigmoid(x):
    return 0.5 * jnp.tanh(0.5 * x) + 0.5


def _merge_kernel(osb_ref, osbm_ref, omla_ref, omlam_ref, odiff_ref, odiffm_ref, zsb_ref, zmla_ref, zdiff_ref,
                  g0_ref, g1_ref, g2_ref, htok_ref, hmeta_ref, bg_ref, wsb_ref, wmla_ref, wdiff_ref, wout_ref,
                  gn_ref, *out_refs, last, n_tok_tiles):
    def branch(o_refs, z_ref, w_ref):
        z = z_ref[...].astype(F32)
        a = _residual_rows(*o_refs, n_tok_tiles).astype(F32) * (z * _sigmoid(z))
        return jnp.dot(a.astype(BF16), w_ref[...], preferred_element_type=F32)

    bg = bg_ref[...]

    def twice_gate(g_ref, n):
        return 1.0 + jnp.tanh(0.5 * (g_ref[...].astype(F32) + bg[:, n * D_MODEL:(n + 1) * D_MODEL]))

    merged = twice_gate(g0_ref, 0) * branch((osb_ref, osbm_ref), zsb_ref, wsb_ref)
    merged += twice_gate(g1_ref, 1) * branch((omla_ref, omlam_ref), zmla_ref, wmla_ref)
    merged += twice_gate(g2_ref, 2) * branch((odiff_ref, odiffm_ref), zdiff_ref, wdiff_ref)
    merged *= 0.5
    h_old = _residual_rows(htok_ref, hmeta_ref, n_tok_tiles)
    h_new = h_old + jnp.dot(merged.astype(BF16), wout_ref[...], preferred_element_type=F32)
    normed = _rms(h_new, gn_ref[...])
    if last:
        out_refs[0][...] = normed
    else:
        out_refs[0][...] = h_new
        out_refs[1][...] = normed.astype(BF16)


def _merge_call(o_sb, o_mla, o_diff, proj, h_tok, h_meta, meta_block, n_tok_tiles, bg, w_sb, w_mla, w_diff, w_out,
                g_next, tm, out_rows, last):
    o_specs = _residual_specs(tm, n_tok_tiles, 0, WIDTH)
    row = lambda w, c: pl.BlockSpec((tm, w), lambda i: (i, c))
    full = lambda a: pl.BlockSpec(a.shape, lambda i: (0, 0))
    if last:
        out_specs = [row(D_MODEL, 0)]
        out_shape = [jax.ShapeDtypeStruct((out_rows, D_MODEL), F32)]
    else:
        out_specs = [row(D_MODEL, 0), row(D_MODEL, 0)]
        out_shape = [jax.ShapeDtypeStruct((out_rows, D_MODEL), F32),
                     jax.ShapeDtypeStruct((out_rows, D_MODEL), BF16)]
    return pl.pallas_call(
        functools.partial(_merge_kernel, last=last, n_tok_tiles=n_tok_tiles),
        grid=(out_rows // tm,),
        in_specs=o_specs * 3
                 + [row(WIDTH, C_SBZ // WIDTH), row(WIDTH, C_MZ // WIDTH), row(WIDTH, C_DZ // WIDTH),
                  row(D_MODEL, 0), row(D_MODEL, 1), row(D_MODEL, 2)]
                 + _residual_specs(tm, n_tok_tiles, meta_block)
                 + [full(bg), full(w_sb), full(w_mla), full(w_diff), full(w_out), full(g_next)],
        out_specs=out_specs,
        out_shape=out_shape,
        compiler_params=_params(("parallel",)),
        name="merge_last" if last else "merge",
    )(*o_sb, *o_mla, *o_diff, proj, proj, proj, proj, proj, proj, h_tok, h_meta, bg, w_sb, w_mla, w_diff, w_out,
      g_next)


def _qk(q, k):
    return lax.dot_general(q, k, (((1,), (1,)), ((), ())), preferred_element_type=F32)


def _iota2(shape, axis):
    return lax.broadcasted_iota(jnp.int32, shape, axis)


def _half_mask(rows, half):
    lane = _iota2((rows, LANES), 1)
    return (lane >= HEAD) if half else (lane < HEAD)


def _pv(p, v):
    return jnp.dot(p.astype(BF16), v, preferred_element_type=F32)


def _incl_tri(n):
    return (_iota2((n, n), 0) >= _iota2((n, n), 1)).astype(BF16)


def _sb_logw(z, tri):
    nz = -z
    lk = jnp.minimum(nz, 0.0) - jnp.log(1.0 + jnp.exp(jnp.minimum(z, nz)))
    cum = jnp.dot(lk.astype(BF16), tri, preferred_element_type=F32)
    return z + cum, cum[:, 0:1]


def _wide(x, width):
    return jnp.concatenate([x] * (width // LANES), axis=-1)


def _row_blocks(t):
    return [pl.ds(r * ROW_CHUNK, ROW_CHUNK) for r in range(t // ROW_CHUNK)]


def _stream_cols(n_streams):
    return [slice((n // 2) * LANES, (n // 2 + 1) * LANES) for n in range(n_streams)]


def _half_queries(q_ref, t, scale):
    qs = []
    for n, c in enumerate(_stream_cols(2 * (q_ref.shape[1] // LANES))):
        qg = q_ref[:, c] if scale is None else q_ref[:, c] * scale
        qs.append(jnp.where(_half_mask(t, n % 2), qg, jnp.zeros_like(qg)))
    return qs


def _join_halves(o_ref, outs, t):
    for g in range(len(outs) // 2):
        o_ref[:, g * LANES:(g + 1) * LANES] = jnp.where(_half_mask(t, 0), outs[2 * g], outs[2 * g + 1]).astype(o_ref.dtype)


def _sb_kernel(q_ref, k_ref, v_ref, km_ref, vm_ref, o_ref, acc_scr, run_scr, *, meta_only):
    i = None if meta_only else pl.program_id(2)
    t = ATT_TILE
    qs = _half_queries(q_ref, t, 0.125)
    n_s = len(qs)
    cols = _stream_cols(n_s)
    tri = _incl_tri(t)
    tri_m = _incl_tri(N_META)
    kms = [km_ref[:, c] for c in cols]
    vms = [vm_ref[:, c] for c in cols]
    blocks = _row_blocks(t)

    def key_step(kbs, vbs, tri_k, masks):
        def scores(n):
            return _qk(qs[n], kbs[n])

        def log_weights(n, z):
            parts = []
            for b in range(len(blocks)):
                zb = z[b * ROW_CHUNK:(b + 1) * ROW_CHUNK]
                tri_b = tri_k
                if masks is not None:
                    live = (b + 1) * ROW_CHUNK
                    zb = jnp.where(masks[b][:, :live], zb[:, :live], NEG_BIG)
                    tri_b = tri_k[:live, :live]
                parts.append(_sb_logw(zb, tri_b))
            return parts

        def fold(n, parts):
            runs = []
            for r, (part, total) in zip(blocks, parts):
                run = run_scr[n, r, :]
                width = part.shape[1]
                w = jnp.exp(part + (_wide(run, width) if width > LANES else run[:, :width]))
                acc_scr[n, r, :] += _pv(w, vbs[n][:width])
                run_scr[n, r, :] = run + total
                runs.append(run + total)
            return runs

        z = {0: scores(0)}
        if n_s > 1:
            z[1] = scores(1)
        lw = {0: log_weights(0, z.pop(0))}
        runs = []
        for n in range(n_s):
            if n + 2 < n_s:
                z[n + 2] = scores(n + 2)
            if n + 1 < n_s:
                lw[n + 1] = log_weights(n + 1, z.pop(n + 1))
            runs += fold(n, lw.pop(n))
        return jnp.max(functools.reduce(jnp.maximum, runs))

    def tiles_at(ref, p):
        rows = pl.ds(pl.multiple_of((i - p) * t, t), t)
        per_group = [ref[rows, cols[2 * g]] for g in range(n_s // 2)]
        return [per_group[n // 2] for n in range(n_s)]

    def tok():
        causal = [_iota2((ROW_CHUNK, t), 1) < _iota2((ROW_CHUNK, t), 0) + b * ROW_CHUNK for b in range(len(blocks))]
        acc_scr[...] = jnp.zeros_like(acc_scr)
        run_scr[...] = jnp.zeros_like(run_scr)
        first = key_step(tiles_at(k_ref, 0), tiles_at(v_ref, 0), tri, causal)

        def alive(c):
            return jnp.logical_and(c[0] <= i, c[1] >= SB_DEAD_RUN)

        def older(c):
            p = c[0]
            return p + 1, key_step(tiles_at(k_ref, p), tiles_at(v_ref, p), tri, None)

        _, largest = lax.while_loop(alive, older, (jnp.int32(1), first))

        @pl.when(largest >= SB_DEAD_RUN)
        def _():
            key_step(kms, vms, tri_m, None)

        return [acc_scr[n] for n in range(n_s)]

    def meta():
        causal_m = _iota2((t, N_META), 1) < _iota2((t, N_META), 0)
        zm = [_qk(qh, km) for qh, km in zip(qs, kms)]
        return [_pv(jnp.exp(_sb_logw(jnp.where(causal_m, z, NEG_BIG), tri_m)[0]), vm) for z, vm in zip(zm, vms)]

    _join_halves(o_ref, meta() if meta_only else tok(), t)


def _sb_scratch(n_s):
    t = ATT_TILE
    return [pltpu.VMEM((n_s, t, LANES), F32), pltpu.VMEM((n_s, t, LANES), F32)]


def _row_max(s):
    return jnp.max(s, axis=-1, keepdims=True)


def _row_sum(p):
    return jnp.sum(p, axis=-1, keepdims=True)


def _softmax_attend(i, qs, k_tile, v_tile, kms, vms, acc_scr, meta_only):
    t = ATT_TILE
    n_s = len(qs)

    def scores_t(keys, q):
        return lax.dot_general(keys, q, (((1,), (1,)), ((), ())), preferred_element_type=F32)

    def pv_t(v, p_t):
        return lax.dot_general(v, p_t.astype(BF16), (((0,), (0,)), ((), ())), preferred_element_type=F32)

    def col_max(s):
        return jnp.max(s, axis=0, keepdims=True)

    def col_sum(p):
        return jnp.sum(p, axis=0, keepdims=True)

    streams = range(n_s)

    def sweep(start, width, stats):
        ss = {n: scores_t(k_tile(n, start, width), qs[n]) for n in range(min(SCORE_LOOKAHEAD, n_s))}
        out, updates = [], []
        for n in streams:
            s = ss.pop(n)
            m_new = jnp.maximum(stats[2 * n], col_max(s))
            alpha = jnp.exp2(stats[2 * n] - m_new)
            p = jnp.exp2(s - m_new)
            out += [m_new, alpha * stats[2 * n + 1] + col_sum(p)]
            updates.append((alpha, pv_t(v_tile(n, start, width), p)))
            if n + SCORE_LOOKAHEAD < n_s:
                ss[n + SCORE_LOOKAHEAD] = scores_t(k_tile(n + SCORE_LOOKAHEAD, start, width), qs[n + SCORE_LOOKAHEAD])
        for n, (alpha, pv) in enumerate(updates):
            acc_scr[n] = alpha * acc_scr[n] + pv
        return tuple(out)

    def tok():
        def first(before):
            start = pl.multiple_of((i - before) * t, (1 + before) * t)
            width = (1 + before) * t
            chunk_ok = _iota2((width, t), 0) // CHUNK <= _iota2((width, t), 1) // CHUNK + before * (t // CHUNK)
            sds = [jnp.where(chunk_ok, scores_t(k_tile(n, start, width), qs[n]), NEG_BIG) for n in streams]
            sms = [scores_t(kms[n], qs[n]) for n in streams]
            ms = [jnp.maximum(col_max(sds[n]), col_max(sms[n])) for n in streams]
            pds = [jnp.exp2(sds[n] - ms[n]) for n in streams]
            pms = [jnp.exp2(sms[n] - ms[n]) for n in streams]
            ls = [col_sum(pds[n]) + col_sum(pms[n]) for n in streams]
            for n in streams:
                acc_scr[n] = pv_t(v_tile(n, start, width), pds[n]) + pv_t(vms[n], pms[n])
            return tuple(x for n in streams for x in (ms[n], ls[n]))

        stats = lax.cond(i % 2 == 1, lambda: first(1), lambda: first(0))
        stats = lax.fori_loop(0, i // 2, lambda j, st: sweep(pl.multiple_of(j * 2 * t, 2 * t), 2 * t, st), stats)
        return [(acc_scr[n] * (1.0 / stats[2 * n + 1])).T for n in range(n_s)]

    def meta():
        sms = [_qk(q, km) for q, km in zip(qs, kms)]
        pms = [jnp.exp2(sm - _row_max(sm)) for sm in sms]
        return [_pv(pm, vm) * (1.0 / _row_sum(pm)) for pm, vm in zip(pms, vms)]

    return meta() if meta_only else tok()


def _softmax_scratch(n_s):
    t = ATT_TILE
    return [pltpu.VMEM((n_s, LANES, t), F32)]


def _mla_kernel(q_ref, k_ref, v_ref, km_ref, vm_ref, o_ref, *scratch, meta_only):
    i = None if meta_only else pl.program_id(2)
    n_s = q_ref.shape[1] // LANES
    qcols = [slice(n * LANES, (n + 1) * LANES) for n in range(n_s)]
    vcols = _stream_cols(n_s)
    outs = _softmax_attend(
        i, [q_ref[:, c] for c in qcols],
        lambda n, st, w: k_ref[pl.ds(st, w), qcols[n]], lambda n, st, w: v_ref[pl.ds(st, w), vcols[n]],
        [km_ref[:, c] for c in qcols], [vm_ref[:, c] for c in vcols], *scratch, meta_only)
    _join_halves(o_ref, outs, ATT_TILE)


def _diff_kernel(q_ref, k_ref, v_ref, km_ref, vm_ref, lam_ref, g_ref, o_ref, *scratch, lam_init, meta_only):
    i = None if meta_only else pl.program_id(2)
    qs = _half_queries(q_ref, ATT_TILE, None)
    cols = _stream_cols(len(qs))
    outs = _softmax_attend(
        i, qs,
        lambda n, st, w: k_ref[pl.ds(st, w), cols[n]], lambda n, st, w: v_ref[pl.ds(st, w), cols[n]],
        [km_ref[:, c] for c in cols], [vm_ref[:, c] for c in cols], *scratch, meta_only)
    lam4 = lam_ref[...]
    lam = (jnp.exp(jnp.sum(lam4[0:1] * lam4[1:2], axis=-1, keepdims=True))
           - jnp.exp(jnp.sum(lam4[2:3] * lam4[3:4], axis=-1, keepdims=True)) + lam_init)
    for g in range(len(qs) // 2):
        o = outs[2 * g] - lam * outs[2 * g + 1]
        o_ref[:, g * LANES:(g + 1) * LANES] = (_rms(o, g_ref[...]) * (1.0 - lam_init)).astype(o_ref.dtype)


def _attn_call(kernel, name, q, qw, qc, k, kw, kc, v, vc, extra, scratch, batch, seq, groups, meta_only):
    t = ATT_TILE
    n_tok_tiles = seq // t
    tok_rows = batch * seq
    meta_q = tok_rows // t
    meta_k = tok_rows // N_META
    vw = WIDTH // groups

    def meta_kv(width, col):
        return pl.BlockSpec((N_META, width), lambda *ids: (meta_k, col + ids[-2 if len(ids) == 3 else 0]))

    if meta_only:
        grid = (groups,)
        in_specs = [pl.BlockSpec((t, qw), lambda g: (meta_q, qc + g)),
                    meta_kv(kw, kc), meta_kv(vw, vc), meta_kv(kw, kc), meta_kv(vw, vc)]
        in_specs += [pl.BlockSpec(a.shape, lambda g: (0, 0)) for a in extra]
        out_spec = pl.BlockSpec((t, vw), lambda g: (0, g))
        out_rows, sem = t, ("parallel",)
    else:
        grid = (batch, groups, n_tok_tiles)
        in_specs = [pl.BlockSpec((t, qw), lambda b, g, i: (b * n_tok_tiles + i, qc + g)),
                    pl.BlockSpec((seq, kw), lambda b, g, i: (b, kc + g)),
                    pl.BlockSpec((seq, vw), lambda b, g, i: (b, vc + g)),
                    meta_kv(kw, kc), meta_kv(vw, vc)]
        in_specs += [pl.BlockSpec(a.shape, lambda b, g, i: (0, 0)) for a in extra]
        out_spec = pl.BlockSpec((t, vw), lambda b, g, i: (b * n_tok_tiles + i, g))
        out_rows, sem = tok_rows, ("parallel", "parallel", "arbitrary")
    return pl.pallas_call(
        functools.partial(kernel, meta_only=meta_only),
        grid=grid,
        in_specs=in_specs,
        out_specs=out_spec,
        out_shape=jax.ShapeDtypeStruct((out_rows, WIDTH), BF16),
        scratch_shapes=scratch,
        compiler_params=_params(sem),
        name=name + ("_meta" if meta_only else ""),
    )(q, k, v, k, v, *extra)


def _perm_w_in(w_in):
    sizes = [512, 512, 512, 512, 384, 256, 32, 512, 512, 512, 512, 512, 3072]
    offs = [0]
    for s in sizes:
        offs.append(offs[-1] + s)
    seg = [w_in[..., offs[n]:offs[n + 1]] for n in range(len(sizes))]
    sb_q, sb_k, sb_v, sb_z, cq, ckv, kr, mz, dq, dk, dv, dz, gate = seg
    pad = lambda n: jnp.zeros(w_in.shape[:-1] + (n,), w_in.dtype)
    cols = [gate, sb_q, sb_k, sb_v, sb_z, dq, dk, dv, dz, mz, ckv, cq, pad(HEAD), kr, pad(LANES - HEAD - MLA_ROPE)]
    return jnp.concatenate(cols, axis=-1).astype(BF16)


def _rope_tables(seq):
    pos = np.concatenate([np.arange(seq) + N_META, np.arange(META_ROWS)]).astype(np.float64)[:, None]

    def tables(dim, starts):
        inv = ROPE_THETA ** (-np.arange(0, dim, 2, dtype=np.float64) / dim)
        ang = pos * inv[None, :]
        cos, sin = np.cos(ang), np.sin(ang)
        half = dim // 2
        c = np.ones((pos.shape[0], LANES))
        s1 = np.zeros((pos.shape[0], LANES))
        s2 = np.zeros((pos.shape[0], LANES))
        for st in starts:
            c[:, st:st + half] = cos
            c[:, st + half:st + dim] = cos
            s1[:, st:st + half] = -sin
            s2[:, st + half:st + dim] = sin
        return [jnp.asarray(a, F32) for a in (c, s1, s2)]

    return tables(MLA_ROPE, [HEAD]) + tables(DIFF_ROT, [0, HEAD])


def _pick(rows, candidates):
    for c in candidates:
        if rows % c == 0:
            return c
    raise ValueError(f"no row tile for {rows}")


def kernel(x, meta_tokens, norm_g, w_in, b_gate, mla_cq_g, mla_ckv_g, mla_w_uq, mla_w_ukv, diff_lambda,
           diff_norm_g, w_o_sb, w_o_mla, w_o_diff, w_out, final_g):
    batch, seq, d = x.shape
    depth = norm_g.shape[0]
    assert d == D_MODEL and seq % ATT_TILE == 0 and meta_tokens.shape == (N_META, D_MODEL)
    tok_rows = batch * seq
    rows = tok_rows + META_ROWS
    tm = ROW_TILE
    tm_proj = _pick(rows, (3328, 1280, 1024, 768, 512, 256))

    n_tok_tiles = tok_rows // tm
    h_tok = x.reshape(tok_rows, d)
    h_meta = jnp.concatenate([meta_tokens.astype(x.dtype), jnp.zeros((META_ROWS - N_META, d), x.dtype)], axis=0)
    meta_block = 0
    wuq = jnp.pad(mla_w_uq.reshape(depth, MLA_Q_RANK, MLA_HEADS, HEAD + MLA_ROPE),
                  ((0, 0), (0, 0), (0, 0), (0, LANES - HEAD - MLA_ROPE))).reshape(depth, MLA_Q_RANK, -1).astype(BF16)
    wukv = mla_w_ukv.reshape(depth, MLA_KV_RANK, MLA_HEADS, 2 * HEAD)
    wuk = jnp.pad(wukv[..., :HEAD], ((0, 0), (0, 0), (0, 0), (0, LANES - HEAD))).reshape(depth, MLA_KV_RANK, -1).astype(BF16)
    wuv = wukv[..., HEAD:].reshape(depth, MLA_KV_RANK, -1).astype(BF16)
    tabs = _rope_tables(seq)
    row2 = lambda a: a.reshape(1, -1)

    hn = _norm_call(h_tok, h_meta, row2(norm_g[0]), tm)
    out = None
    for l in range(depth):
        last = l == depth - 1
        proj = _proj_call(hn, _perm_w_in(w_in[l]), tm_proj)
        q_m, k_m, v_m, q_d, k_d = _prep_call(proj, row2(mla_cq_g[l]), row2(mla_ckv_g[l]), wuq[l], wuk[l], wuv[l],
                                             tabs, tm, tok_rows, seq)
        gw = ATT_GROUPS * LANES
        n_s = 2 * ATT_GROUPS
        lam_init = 0.8 - 0.6 * math.exp(-0.3 * l)
        passes = [
            [_attn_call(_sb_kernel, "sb_attn", proj, gw, C_SBQ // gw, proj, gw, C_SBK // gw, proj, C_SBV // gw, [],
                        _sb_scratch(n_s), batch, seq, WIDTH // gw, meta_only),
             _attn_call(_mla_kernel, "mla_attn", q_m, 2 * gw, 0, k_m, 2 * gw, 0, v_m, 0, [],
                        _softmax_scratch(n_s), batch, seq, WIDTH // gw, meta_only),
             _attn_call(functools.partial(_diff_kernel, lam_init=lam_init), "diff_attn",
                        q_d, gw, 0, k_d, gw, 0, proj, C_DV // gw,
                        [diff_lambda[l].astype(F32), row2(diff_norm_g[l])], _softmax_scratch(n_s), batch, seq,
                        WIDTH // gw, meta_only)]
            for meta_only in (False, True)]
        o_sb, o_mla, o_diff = zip(*passes)
        g_next = row2(final_g if last else norm_g[l + 1])
        res = _merge_call(o_sb, o_mla, o_diff, proj, h_tok, h_meta, meta_block, n_tok_tiles, row2(b_gate[l]),
                          w_o_sb[l].astype(BF16), w_o_mla[l].astype(BF16), w_o_diff[l].astype(BF16),
                          w_out[l].astype(BF16), g_next, tm, tok_rows if last else rows, last)
        if last:
            out = res[0]
        else:
            h, hn = res
            h_tok, h_meta, meta_block = h, h, n_tok_tiles
    return out.reshape(batch, seq, d)
```
